```python
import math
import jax
import jax.numpy as jnp
from jax import lax
import numpy as np

D_MODEL = 1024
BATCH = 32
SEQ = 2048
DEPTH = 4

GRID_W = 64
CTX_LEN = 256
N_MIXERS = 3
EPS = 1e-6
S5_GROUP = 16
S5_GROUPS = D_MODEL // S5_GROUP
S5_STATE = 64
ML_HEADS = 4
ML_HEAD_DIM = D_MODEL // ML_HEADS
ML_CHUNK = 64
ML_CONV = 3
NA_HEADS = 16
NA_HEAD_DIM = D_MODEL // NA_HEADS
NA_WIN_ROWS = 8
NA_WIN_COLS = 16
ROPE_BASE = 10000.0
D_FF = 128 * ((8 * D_MODEL // 3 + 127) // 128)
FFN_CONV = 3
F32 = jnp.float32

kernel_name = 'hybrid_s5_mlstm_natten_flow_block'


def _rmsnorm(x, g):
    xf = x.astype(F32)
    y = xf * lax.rsqrt(jnp.mean(xf * xf, axis=-1, keepdims=True) + EPS)
    return (y * g.astype(F32)).astype(x.dtype)


def _modulation(cvec, w, b):
    return jnp.split(jax.nn.silu(cvec) @ w + b, 6, axis=-1)


def _dwconv(x, w):
    k = w.shape[0]
    return lax.conv_general_dilated(
        x, w[:, None, :].astype(x.dtype), window_strides=(1,),
        padding=[(k // 2, k // 2)], dimension_numbers=('NWC', 'WIO', 'NWC'),
        feature_group_count=x.shape[-1])


def _conv_ffn(h, w_in, conv_w, w_out):
    u = _dwconv(h @ w_in, conv_w)
    a, g = jnp.split(u, 2, axis=-1)
    return (a * jax.nn.silu(g)) @ w_out


def _rope_axis(x, pos):
    nf = x.shape[-1] // 2
    inv = ROPE_BASE ** (-jnp.arange(nf, dtype=F32) / nf)
    ang = pos.astype(F32)[:, None] * inv[None, :]
    cos, sin = jnp.cos(ang), jnp.sin(ang)
    x1, x2 = x[..., :nf], x[..., nf:]
    return jnp.concatenate([x1 * cos - x2 * sin, x1 * sin + x2 * cos], axis=-1)


def _rope_2d(x):
    t = jnp.arange(x.shape[2], dtype=jnp.int32)
    half = x.shape[-1] // 2
    return jnp.concatenate([_rope_axis(x[..., :half], t // GRID_W),
                            _rope_axis(x[..., half:], t % GRID_W)], axis=-1)


def _s5_discretise(lam_re, lam_im, log_dt, b_re, b_im):
    lam_re, lam_im = lam_re.astype(F32), lam_im.astype(F32)
    b_re, b_im = b_re.astype(F32), b_im.astype(F32)
    dt = jnp.exp(log_dt.astype(F32))[:, None]
    lr = jnp.minimum(lam_re, -1e-4)
    er = jnp.exp(lr * dt)
    ang = lam_im * dt
    a_re, a_im = er * jnp.cos(ang), er * jnp.sin(ang)
    den = lr * lr + lam_im * lam_im
    nr = a_re - 1.0
    q_re = (nr * lr + a_im * lam_im) / den
    q_im = (a_im * lr - nr * lam_im) / den
    bb_re = q_re[..., None] * b_re - q_im[..., None] * b_im
    bb_im = q_re[..., None] * b_im + q_im[..., None] * b_re
    return a_re, a_im, bb_re, bb_im


def _cmul_combine(e1, e2):
    a1r, a1i, b1r, b1i = e1
    a2r, a2i, b2r, b2i = e2
    return (a2r * a1r - a2i * a1i, a2r * a1i + a2i * a1r,
            a2r * b1r - a2i * b1i + b2r, a2r * b1i + a2i * b1r + b2i)


def _s5_states(u, a_re, a_im, bb_re, bb_im, h0):
    bu_re = jnp.einsum('lbgh,gph->lbgp', u, bb_re)
    bu_im = jnp.einsum('lbgh,gph->lbgp', u, bb_im)
    if h0 is not None:
        h_re, h_im = h0
        bu_re = bu_re.at[0].add(a_re * h_re - a_im * h_im)
        bu_im = bu_im.at[0].add(a_re * h_im + a_im * h_re)
    shape = (u.shape[0], 1) + a_re.shape
    ar, ai = jnp.broadcast_to(a_re, shape), jnp.broadcast_to(a_im, shape)
    _, _, x_re, x_im = lax.associative_scan(_cmul_combine, (ar, ai, bu_re, bu_im), axis=0)
    return x_re, x_im


def _s5_readout(x_re, x_im, c_re, c_im):
    return (jnp.einsum('lbgp,ghp->lbgh', x_re, c_re.astype(F32))
            - jnp.einsum('lbgp,ghp->lbgh', x_im, c_im.astype(F32)))


def _s5_mixer(h_ctx, h_lat, lam_re, lam_im, log_dt, b_re, b_im, c_re, c_im,
              d_skip, glu_w, glu_b, need_ctx):
    def to_groups(h):
        return jnp.swapaxes(h.astype(F32), 0, 1).reshape(h.shape[1], h.shape[0], S5_GROUPS, S5_GROUP)

    def finish(y, u, dtype):
        y = y + d_skip.astype(F32).reshape(S5_GROUPS, S5_GROUP) * u
        y = jax.nn.gelu(jnp.swapaxes(y.reshape(u.shape[0], u.shape[1], D_MODEL), 0, 1))
        a, g = jnp.split(y @ glu_w.astype(F32) + glu_b.astype(F32), 2, axis=-1)
        return (a * jax.nn.sigmoid(g)).astype(dtype)

    u_ctx, u_lat = to_groups(h_ctx), to_groups(h_lat)
    y_lat, y_ctx = 0.0, 0.0
    for dirn in range(2):
        a_re, a_im, bb_re, bb_im = _s5_discretise(lam_re[dirn], lam_im[dirn], log_dt[dirn],
                                                  b_re[dirn], b_im[dirn])
        rev = (lambda t: t) if dirn == 0 else (lambda t: t[::-1])
        xc_re, xc_im = _s5_states(rev(u_ctx), a_re, a_im, bb_re, bb_im, None)
        xl_re, xl_im = _s5_states(rev(u_lat), a_re, a_im, bb_re, bb_im, (xc_re[-1], xc_im[-1]))
        y_lat = y_lat + rev(_s5_readout(xl_re, xl_im, c_re[dirn], c_im[dirn]))
        if need_ctx:
            y_ctx = y_ctx + rev(_s5_readout(xc_re, xc_im, c_re[dirn], c_im[dirn]))
    out_lat = finish(y_lat, u_lat, h_lat.dtype)
    out_ctx = finish(y_ctx, u_ctx, h_ctx.dtype) if need_ctx else None
    return out_ctx, out_lat


def _mlstm_chunk_scan(q, k, v, i_pre, logf, state):
    bsz, nh, seq, dh = q.shape
    nc = seq // ML_CHUNK

    def chunks(t):
        return jnp.moveaxis(t.reshape((bsz, nh, nc, ML_CHUNK) + t.shape[3:]), 2, 0)

    tril = jnp.tril(jnp.ones((ML_CHUNK, ML_CHUNK), dtype=bool))

    def step(carry, inp):
        c_mat, n_vec, m = carry
        qc, kc, vc, ic, fc = inp
        b = jnp.cumsum(fc, axis=-1)
        log_d = jnp.where(tril, b[..., :, None] - b[..., None, :] + ic[..., None, :], -jnp.inf)
        log_inter = b + m[..., None]
        m_t = jnp.maximum(log_inter, jnp.max(log_d, axis=-1))
        w_intra = jnp.einsum('bhtd,bhsd->bhts', qc, kc) * jnp.exp(log_d - m_t[..., None])
        w_inter = jnp.exp(log_inter - m_t)
        num = (jnp.einsum('bhts,bhsv->bhtv', w_intra, vc)
               + w_inter[..., None] * jnp.einsum('bhvk,bhtk->bhtv', c_mat, qc))
        den = jnp.sum(w_intra, axis=-1) + w_inter * jnp.einsum('bhk,bhtk->bht', n_vec, qc)
        h = num / jnp.maximum(jnp.abs(den), jnp.exp(-m_t))[..., None]
        b_end = b[..., -1]
        log_w = b_end[..., None] - b + ic
        m_new = jnp.maximum(b_end + m, jnp.max(log_w, axis=-1))
        w = jnp.exp(log_w - m_new[..., None])
        decay = jnp.exp(b_end + m - m_new)
        c_new = decay[..., None, None] * c_mat + jnp.einsum('bhs,bhsv,bhsk->bhvk', w, vc, kc)
        n_new = decay[..., None] * n_vec + jnp.einsum('bhs,bhsk->bhk', w, kc)
        return (c_new, n_new, m_new), h

    state, hs = lax.scan(step, state, (chunks(q), chunks(k), chunks(v), chunks(i_pre), chunks(logf)))
    return jnp.moveaxis(hs, 0, 2).reshape(bsz, nh, seq, dh), state


def _mlstm_mixer(h_ctx, h_lat, w_in, conv_w, w_if, b_if, norm_g, w_out, need_ctx):
    D, H, dh = D_MODEL, ML_HEADS, ML_HEAD_DIM

    def heads(t):
        return t.reshape(t.shape[0], t.shape[1], H, dh).transpose(0, 2, 1, 3)

    def prep(h, rotate):
        hf = h.astype(F32)
        proj = hf @ w_in.astype(F32)
        qk = jax.nn.silu(_dwconv(proj[..., :2 * D], conv_w.astype(F32)))
        q, k = heads(qk[..., :D]), heads(qk[..., D:])
        if rotate:
            q, k = _rope_2d(q), _rope_2d(k)
        return hf, q * dh ** -0.5, k, heads(proj[..., 2 * D:3 * D]), jax.nn.sigmoid(proj[..., 3 * D:])

    def finish(h_sum, o, dtype):
        hn = h_sum * lax.rsqrt(jnp.mean(h_sum * h_sum, axis=-1, keepdims=True) + EPS)
        hn = hn.transpose(0, 2, 1, 3).reshape(o.shape) * norm_g.astype(F32)
        return ((o * hn) @ w_out.astype(F32)).astype(dtype)

    hc, qc, kc, vc, oc = prep(h_ctx, False)
    hl, ql, kl, vl, ol = prep(h_lat, True)
    bsz = h_lat.shape[0]
    h_lat_sum, h_ctx_sum = 0.0, 0.0
    for dirn in range(2):
        wg, bg = w_if[dirn].astype(F32), b_if[dirn].astype(F32)

        def gates(hf):
            g = hf @ wg + bg
            return (jnp.swapaxes(g[..., :H], 1, 2),
                    jnp.swapaxes(jax.nn.log_sigmoid(g[..., H:]), 1, 2))

        rev = (lambda t: t) if dirn == 0 else (lambda t: jnp.flip(t, axis=2))
        ic, fc = gates(hc)
        il, fl = gates(hl)
        state0 = (jnp.zeros((bsz, H, dh, dh), F32), jnp.zeros((bsz, H, dh), F32),
                  jnp.zeros((bsz, H), F32))
        out_c, state_c = _mlstm_chunk_scan(rev(qc), rev(kc), rev(vc), rev(ic), rev(fc), state0)
        out_l, _ = _mlstm_chunk_scan(rev(ql), rev(kl), rev(vl), rev(il), rev(fl), state_c)
        h_lat_sum = h_lat_sum + rev(out_l)
        if need_ctx:
            h_ctx_sum = h_ctx_sum + rev(out_c)
    out_lat = finish(h_lat_sum, ol, h_lat.dtype)
    out_ctx = finish(h_ctx_sum, oc, h_ctx.dtype) if need_ctx else None
    return out_ctx, out_lat


def _na_mixer(h_ctx, h_lat, w_qkv, rpb, w_out, need_ctx):
    H, dh = NA_HEADS, NA_HEAD_DIM
    bsz, seq, _ = h_lat.shape
    rows = seq // GRID_W
    kr = min(NA_WIN_ROWS, rows)
    n_loc = kr * GRID_W
    scale = dh ** -0.5
    rpb = rpb.astype(F32)

    def heads(h):
        qkv = (h.astype(F32) @ w_qkv.astype(F32)).reshape(h.shape[0], h.shape[1], 3, H, dh)
        q, k, v = [qkv[:, :, j].transpose(0, 2, 1, 3) for j in range(3)]
        return q * scale, k, v

    q_c, k_c, v_c = heads(h_ctx)
    q_l, k_l, v_l = heads(h_lat)
    k_grid = k_l.reshape(bsz, H, rows, GRID_W, dh)
    v_grid = v_l.reshape(bsz, H, rows, GRID_W, dh)
    q_rows = jnp.moveaxis(q_l.reshape(bsz, H, rows, GRID_W, dh), 2, 0)
    cols = jnp.arange(GRID_W, dtype=jnp.int32)
    c0 = jnp.clip(cols - NA_WIN_COLS // 2, 0, GRID_W - NA_WIN_COLS)
    col_mask = (cols[None, :] >= c0[:, None]) & (cols[None, :] < c0[:, None] + NA_WIN_COLS)
    dc_idx = jnp.clip(cols[None, :] - cols[:, None], 1 - NA_WIN_COLS, NA_WIN_COLS - 1) + NA_WIN_COLS - 1

    def row_block(args):
        r, q_r = args
        r0 = jnp.clip(r - kr // 2, 0, rows - kr)
        k_b = lax.dynamic_slice_in_dim(k_grid, r0, kr, axis=2)
        v_b = lax.dynamic_slice_in_dim(v_grid, r0, kr, axis=2)
        dr_idx = r0 + jnp.arange(kr, dtype=jnp.int32) - r + NA_WIN_ROWS - 1
        bias = rpb[:, dr_idx[None, :, None], dc_idx[:, None, :]]
        s_loc = jnp.einsum('bhqd,bhrkd->bhqrk', q_r, k_b) + bias
        s_loc = jnp.where(col_mask[:, None, :], s_loc, -jnp.inf).reshape(bsz, H, GRID_W, n_loc)
        s_ctx = jnp.einsum('bhqd,bhcd->bhqc', q_r, k_c)
        p = jax.nn.softmax(jnp.concatenate([s_loc, s_ctx], axis=-1), axis=-1)
        return (jnp.einsum('bhqn,bhnd->bhqd', p[..., :n_loc], v_b.reshape(bsz, H, n_loc, dh))
                + jnp.einsum('bhqc,bhcd->bhqd', p[..., n_loc:], v_c))

    o_rows = lax.map(row_block, (jnp.arange(rows, dtype=jnp.int32), q_rows))
    o_lat = o_rows.transpose(1, 0, 3, 2, 4).reshape(bsz, seq, D_MODEL)
    out_lat = (o_lat @ w_out.astype(F32)).astype(h_lat.dtype)
    out_ctx = None
    if need_ctx:
        p_c = jax.nn.softmax(jnp.einsum('bhqd,bhkd->bhqk', q_c, k_c), axis=-1)
        o_ctx = jnp.einsum('bhqk,bhkd->bhqd', p_c, v_c).transpose(0, 2, 1, 3)
        o_ctx = o_ctx.reshape(h_ctx.shape[0], h_ctx.shape[1], D_MODEL)
        out_ctx = (o_ctx @ w_out.astype(F32)).astype(h_ctx.dtype)
    return out_ctx, out_lat


def setup_inputs(seed: int = 0) -> dict:
    key = jax.random.key(seed)
    ks = iter(jax.random.split(key, 40))

    def nrm(shape, std):
        return std * jax.random.normal(next(ks), shape, F32)

    D, F = D_MODEL, D_FF
    G, P, GC = S5_GROUPS, S5_STATE, S5_GROUP
    H = ML_HEADS
    n_a = len(range(0, DEPTH, N_MIXERS))
    n_b = len(range(1, DEPTH, N_MIXERS))
    n_c = len(range(2, DEPTH, N_MIXERS))
    inp = {}
    inp['x'] = nrm((BATCH, SEQ, D), 1.0)
    inp['c'] = nrm((BATCH, D), 1.0)
    inp['ctx'] = nrm((BATCH, CTX_LEN, D), 1.0)
    inp['c_ctx'] = nrm((D,), 1.0)
    inp['ada_w'] = nrm((DEPTH, D, 6 * D), 0.5 * D ** -0.5)
    inp['ada_b'] = nrm((DEPTH, 6 * D), 0.02)
    inp['norm1_g'] = 1.0 + nrm((DEPTH, D), 0.02)
    inp['norm2_g'] = 1.0 + nrm((DEPTH, D), 0.02)
    inp['ffn_w_in'] = nrm((DEPTH, D, 2 * F), D ** -0.5)
    inp['ffn_conv'] = nrm((DEPTH, FFN_CONV, 2 * F), FFN_CONV ** -0.5)
    inp['ffn_w_out'] = nrm((DEPTH, F, D), F ** -0.5)
    inp['s5_lam_re'] = -0.5 + nrm((n_a, 2, G, P), 0.01)
    inp['s5_lam_im'] = math.pi * jnp.arange(P, dtype=F32) + nrm((n_a, 2, G, P), 0.01)
    inp['s5_log_dt'] = jax.random.uniform(next(ks), (n_a, 2, G), F32, math.log(1e-3), math.log(1e-1))
    inp['s5_b_re'] = nrm((n_a, 2, G, P, GC), (2 * GC) ** -0.5)
    inp['s5_b_im'] = nrm((n_a, 2, G, P, GC), (2 * GC) ** -0.5)
    inp['s5_c_re'] = nrm((n_a, 2, G, GC, P), 0.5)
    inp['s5_c_im'] = nrm((n_a, 2, G, GC, P), 0.5)
    inp['s5_d'] = nrm((n_a, D), 0.5)
    inp['s5_glu_w'] = nrm((n_a, D, 2 * D), D ** -0.5)
    inp['s5_glu_b'] = nrm((n_a, 2 * D), 0.02)
    inp['ml_w_in'] = nrm((n_b, D, 4 * D), D ** -0.5)
    inp['ml_conv'] = nrm((n_b, ML_CONV, 2 * D), ML_CONV ** -0.5)
    inp['ml_w_if'] = nrm((n_b, 2, D, 2 * H), 0.1 * D ** -0.5)
    inp['ml_b_if'] = jnp.concatenate(
        [nrm((n_b, 2, H), 0.1), jnp.linspace(3.0, 6.0, H, dtype=F32) + nrm((n_b, 2, H), 0.1)], axis=-1)
    inp['ml_norm_g'] = 1.0 + nrm((n_b, D), 0.02)
    inp['ml_w_out'] = nrm((n_b, D, D), D ** -0.5)
    inp['na_w_qkv'] = nrm((n_c, D, 3 * D), D ** -0.5)
    inp['na_rpb'] = nrm((n_c, NA_HEADS, 2 * NA_WIN_ROWS - 1, 2 * NA_WIN_COLS - 1), 0.1)
    inp['na_w_out'] = nrm((n_c, D, D), D ** -0.5)
    inp['final_g'] = 1.0 + nrm((D,), 0.02)
    return inp


def reference(x, c, ctx, c_ctx, ada_w, ada_b, norm1_g, norm2_g, ffn_w_in, ffn_conv, ffn_w_out,
              s5_lam_re, s5_lam_im, s5_log_dt, s5_b_re, s5_b_im, s5_c_re, s5_c_im, s5_d,
              s5_glu_w, s5_glu_b, ml_w_in, ml_conv, ml_w_if, ml_b_if, ml_norm_g, ml_w_out,
              na_w_qkv, na_rpb, na_w_out, final_g):
    for i in range(DEPTH):
        need_ctx = i < DEPTH - 1
        sh1, sc1, g1, sh2, sc2, g2 = [m[:, None, :] for m in _modulation(c, ada_w[i], ada_b[i])]
        csh1, csc1, cg1, csh2, csc2, cg2 = _modulation(c_ctx, ada_w[i], ada_b[i])
        h_lat = _rmsnorm(x, norm1_g[i]) * (1 + sc1) + sh1
        h_ctx = _rmsnorm(ctx, norm1_g[i]) * (1 + csc1) + csh1
        kind, j = i % N_MIXERS, i // N_MIXERS
        if kind == 0:
            y_ctx, y_lat = _s5_mixer(h_ctx, h_lat, s5_lam_re[j], s5_lam_im[j], s5_log_dt[j],
                                     s5_b_re[j], s5_b_im[j], s5_c_re[j], s5_c_im[j], s5_d[j],
                                     s5_glu_w[j], s5_glu_b[j], need_ctx)
        elif kind == 1:
            y_ctx, y_lat = _mlstm_mixer(h_ctx, h_lat, ml_w_in[j], ml_conv[j], ml_w_if[j], ml_b_if[j],
                                        ml_norm_g[j], ml_w_out[j], need_ctx)
        else:
            y_ctx, y_lat = _na_mixer(h_ctx, h_lat, na_w_qkv[j], na_rpb[j], na_w_out[j], need_ctx)
        x = x + g1 * y_lat
        x = x + g2 * _conv_ffn(_rmsnorm(x, norm2_g[i]) * (1 + sc2) + sh2,
                               ffn_w_in[i], ffn_conv[i], ffn_w_out[i])
        if need_ctx:
            ctx = ctx + cg1 * y_ctx
            ctx = ctx + cg2 * _conv_ffn(_rmsnorm(ctx, norm2_g[i]) * (1 + csc2) + csh2,
                                        ffn_w_in[i], ffn_conv[i], ffn_w_out[i])
    return _rmsnorm(x, final_g)
```

```python
import functools
import math

import jax
import jax.numpy as jnp
from jax import lax
from jax.experimental import pallas as pl
from jax.experimental.pallas import tpu as pltpu

F32 = jnp.float32
MXU_DTYPE = jnp.bfloat16
EPS = 1e-6
NEG = -1e30
TS = 256
HALO = 8
GRID_W = 64
ROPE_BASE = 10000.0
S5_GROUP = 16
S5_STATE = 64
S5_T = 16
ML_HEADS = 4
NA_HEADS = 16
NA_WIN_ROWS = 8
NA_WIN_COLS = 16
VMEM_LIMIT = 56 * 1024 * 1024


def _cparams(sem):
    return pltpu.CompilerParams(dimension_semantics=sem, vmem_limit_bytes=VMEM_LIMIT)


def _dot(a, b):
    return jnp.dot(a.astype(MXU_DTYPE), b.astype(MXU_DTYPE), preferred_element_type=F32)


def _dot_nt(a, b):
    return lax.dot_general(a.astype(MXU_DTYPE), b.astype(MXU_DTYPE), (((1,), (1,)), ((), ())),
                           preferred_element_type=F32)


def _dot_tn(a, b):
    return lax.dot_general(a.astype(MXU_DTYPE), b.astype(MXU_DTYPE), (((0,), (0,)), ((), ())),
                           preferred_element_type=F32)


def _norm_mod(x, g, sc, sh):
    ms = jnp.mean(x * x, axis=-1, keepdims=True)
    return (x * lax.rsqrt(ms + EPS) * g) * (1.0 + sc) + sh


def _sigmoid(x):
    return 1.0 / (1.0 + jnp.exp(-x))


def _log_sigmoid(x):
    return jnp.minimum(x, 0.0) - jnp.log1p(jnp.exp(-jnp.abs(x)))


def _shift_rows(u, prev_row, next_row):
    n = u.shape[0]
    rows = lax.broadcasted_iota(jnp.int32, u.shape, 0)
    dn = jnp.where(rows == 0, prev_row, pltpu.roll(u, 1, axis=0))
    up = jnp.where(rows == n - 1, next_row, pltpu.roll(u, n - 1, axis=0))
    return dn, up


def _halo_valid(t, nt):
    prev_ok = (t >= 2).astype(F32)
    next_ok = jnp.logical_and(t >= 1, t <= nt - 2).astype(F32)
    return prev_ok, next_ok


def _tile_specs(D, nt):
    per = TS // HALO
    main = pl.BlockSpec((1, TS, D), lambda b, t: (b, t, 0))
    prev = pl.BlockSpec((1, HALO, D), lambda b, t: (b, jnp.maximum(t * per - 1, 0), 0))
    nxt = pl.BlockSpec((1, HALO, D), lambda b, t: (b, jnp.minimum((t + 1) * per, nt * per - 1), 0))
    mod = pl.BlockSpec((1, 1, 6, D), lambda b, t: (b, jnp.minimum(t, 1), 0, 0))
    return main, prev, nxt, mod


def _const_spec(shape):
    nd = len(shape)
    return pl.BlockSpec(shape, lambda *_: (0,) * nd)


def _mod_kernel(c_ref, w_ref, b_ref, o_ref):
    cv = c_ref[...]
    o_ref[0] = _dot(cv * _sigmoid(cv), w_ref[0]) + b_ref[0]


def _modulation_all(c, c_ctx, ada_w, ada_b):
    depth, D, D6 = ada_w.shape
    B = c.shape[0]
    rows = ((B + 1 + 7) // 8) * 8
    cc = jnp.zeros((rows, D), F32).at[:B].set(c).at[B].set(c_ctx)
    nb = 4
    out = pl.pallas_call(
        _mod_kernel,
        grid=(depth, nb),
        in_specs=[pl.BlockSpec((rows, D), lambda i, n: (0, 0)),
                  pl.BlockSpec((1, D, D6 // nb), lambda i, n: (i, 0, n)),
                  pl.BlockSpec((1, 1, D6 // nb), lambda i, n: (i, 0, n))],
        out_specs=pl.BlockSpec((1, rows, D6 // nb), lambda i, n: (i, 0, n)),
        out_shape=jax.ShapeDtypeStruct((depth, rows, D6), F32),
        compiler_params=_cparams(("parallel", "parallel")),
        name="adaln_modulation",
    )(cc, ada_w, ada_b.reshape(depth, 1, D6))
    lat = out[:, :B].reshape(depth, B, 1, 6, D)
    cx = jnp.broadcast_to(out[:, B].reshape(depth, 1, 1, 6, D), (depth, B, 1, 6, D))
    return jnp.concatenate([cx, lat], axis=2)


def _ffn_kernel(x_ref, xp_ref, xn_ref, mod_ref, g_ref, win_ref, cw_ref, wout_ref, o_ref, *, F, FC, nt):
    t = pl.program_id(1)
    mod = mod_ref[0, 0]
    sh, sc, gate = mod[3:4], mod[4:5], mod[5:6]
    g = g_ref[...]
    x = x_ref[0]
    h = _norm_mod(x, g, sc, sh).astype(MXU_DTYPE)
    hh = _norm_mod(jnp.concatenate([xp_ref[0], xn_ref[0]], axis=0), g, sc, sh).astype(MXU_DTYPE)
    prev_ok, next_ok = _halo_valid(t, nt)

    def conv_cols(off):
        w = win_ref[:, off:off + FC]
        u = _dot(h, w)
        uh = _dot(hh, w)
        cw = cw_ref[:, off:off + FC]
        dn, up = _shift_rows(u, uh[HALO - 1:HALO] * prev_ok, uh[HALO:HALO + 1] * next_ok)
        return cw[0:1] * dn + cw[1:2] * u + cw[2:3] * up

    acc = jnp.zeros(x.shape, F32)
    for c in range(F // FC):
        a = conv_cols(c * FC)
        gg = conv_cols(F + c * FC)
        act = a * (gg * _sigmoid(gg))
        acc = acc + _dot(act, wout_ref[c * FC:(c + 1) * FC, :])
    o_ref[0] = x + gate * acc


def _ffn_layer(xs, mod, g, w_in, conv_w, w_out):
    B, S, D = xs.shape
    F = w_out.shape[0]
    nt = S // TS
    main, prev, nxt, mspec = _tile_specs(D, nt)
    kern = functools.partial(_ffn_kernel, F=F, FC=256, nt=nt)
    return pl.pallas_call(
        kern,
        grid=(B, nt),
        in_specs=[main, prev, nxt, mspec, _const_spec((1, D)), _const_spec((D, 2 * F)),
                  _const_spec((3, 2 * F)), _const_spec((F, D))],
        out_specs=main,
        out_shape=jax.ShapeDtypeStruct(xs.shape, F32),
        compiler_params=_cparams(("parallel", "parallel")),
        name="conv_ffn",
    )(xs, xs, xs, mod, g.reshape(1, D), w_in.astype(MXU_DTYPE), conv_w, w_out.astype(MXU_DTYPE))


def _s5_pre_kernel(x_ref, mod_ref, g_ref, o_ref):
    mod = mod_ref[0, 0]
    o_ref[0] = _norm_mod(x_ref[0], g_ref[...], mod[1:2], mod[0:1]).astype(o_ref.dtype)


def _s5_core_kernel(u_ref, w1_ref, w2f_ref, w2b_ref, a16_ref, d_ref, o_ref, z_ref, xpf_ref, xpb_ref,
                    *, nb, nctx, nchunks, rb):
    R = nchunks * nb
    w1 = w1_ref[0]

    def proj(i, carry):
        r0 = pl.multiple_of(i * rb, rb)
        z_ref[pl.ds(r0, rb), :] = _dot(u_ref[0, pl.ds(r0, rb), :], w1)
        return carry

    lax.fori_loop(0, R // rb, proj, 0)

    a16 = a16_ref[0]
    ar, ai = a16[:, :128], a16[:, 128:]
    is_f = lax.broadcasted_iota(jnp.int32, (nb, 128), 1) < S5_STATE

    def step(k, carry):
        xr, xi = carry
        cb = jnp.where(k < nctx, nctx - 1 - k, nchunks - 1 - (k - nctx))
        rf = pl.multiple_of(k * nb, nb)
        rbk = pl.multiple_of(cb * nb, nb)
        st = jnp.concatenate([xr, xi], axis=1)
        xpf_ref[pl.ds(rf, nb), :] = st
        xpb_ref[pl.ds(rbk, nb), :] = st
        ef = z_ref[pl.ds(rf, nb), 256:512]
        eb = z_ref[pl.ds(rbk, nb), 256:512]
        er = jnp.where(is_f, ef[:, :128], eb[:, :128])
        ei = jnp.where(is_f, ef[:, 128:], eb[:, 128:])
        return ar * xr - ai * xi + er, ar * xi + ai * xr + ei

    zero = jnp.zeros((nb, 128), F32)
    lax.fori_loop(0, nchunks, step, (zero, zero))

    w2f, w2b, dsk = w2f_ref[0], w2b_ref[0], d_ref[0]

    def finish(i, carry):
        r0 = pl.multiple_of(i * rb, rb)
        rows = pl.ds(r0, rb)
        y = z_ref[rows, 0:256] + _dot(xpf_ref[rows, :], w2f) + _dot(xpb_ref[rows, :], w2b)
        y = y + dsk * u_ref[0, rows, :].astype(F32)
        o_ref[0, rows, :] = jax.nn.gelu(y).astype(o_ref.dtype)
        return carry

    lax.fori_loop(0, R // rb, finish, 0)


def _s5_glu_kernel(x_ref, y_ref, mod_ref, w_ref, b_ref, o_ref, *, D):
    gate = mod_ref[0, 0][2:3]
    z = _dot(y_ref[0], w_ref[...]) + b_ref[...]
    o_ref[0] = x_ref[0] + gate * (z[:, :D] * _sigmoid(z[:, D:]))


def _s5_weights(lam_re, lam_im, log_dt, b_re, b_im, c_re, c_im, d_skip):
    T, P, GC = S5_T, S5_STATE, S5_GROUP
    G = lam_re.shape[1]
    lam_re, lam_im = lam_re.astype(F32), lam_im.astype(F32)
    dt = jnp.exp(log_dt.astype(F32))[..., None]
    lr = jnp.minimum(lam_re, -1e-4)
    er = jnp.exp(lr * dt)
    ang = lam_im * dt
    a_re, a_im = er * jnp.cos(ang), er * jnp.sin(ang)
    den = lr * lr + lam_im * lam_im
    nr = a_re - 1.0
    q_re = (nr * lr + a_im * lam_im) / den
    q_im = (a_im * lr - nr * lam_im) / den
    bb_re = q_re[..., None] * b_re - q_im[..., None] * b_im
    bb_im = q_re[..., None] * b_im + q_im[..., None] * b_re
    tau = jnp.arange(T + 1, dtype=F32)[:, None, None, None]
    mag = jnp.exp(lr * dt * tau)
    p_re, p_im = mag * jnp.cos(ang * tau), mag * jnp.sin(ang * tau)
    c_re, c_im = c_re.astype(F32), c_im.astype(F32)
    m_re = c_re[None] * p_re[:, :, :, None, :] - c_im[None] * p_im[:, :, :, None, :]
    m_im = c_re[None] * p_im[:, :, :, None, :] + c_im[None] * p_re[:, :, :, None, :]
    hi = lax.Precision.HIGHEST
    kern = (jnp.einsum('tdghp,dgpi->tdghi', m_re[:T], bb_re, precision=hi)
            - jnp.einsum('tdghp,dgpi->tdghi', m_im[:T], bb_im, precision=hi))
    s_i = jnp.arange(T)[:, None]
    t_i = jnp.arange(T)[None, :]
    kf = kern[jnp.clip(t_i - s_i, 0, T - 1), 0]
    kb = kern[jnp.clip(s_i - t_i, 0, T - 1), 1]
    kf = jnp.where((t_i >= s_i)[:, :, None, None, None], kf, 0.0)
    kb = jnp.where((s_i >= t_i)[:, :, None, None, None], kb, 0.0)
    w_intra = jnp.transpose(kf + kb, (2, 0, 4, 1, 3)).reshape(G, T * GC, T * GC)
    pf_re, pf_im = p_re[T - 1 - jnp.arange(T), 0], p_im[T - 1 - jnp.arange(T), 0]
    pb_re, pb_im = p_re[:T, 1], p_im[:T, 1]

    def local(pr, pi, br, bi):
        re = pr[..., None] * br[None] - pi[..., None] * bi[None]
        im = pr[..., None] * bi[None] + pi[..., None] * br[None]
        tr = lambda v: jnp.transpose(v, (1, 0, 3, 2)).reshape(G, T * GC, P)
        return tr(re), tr(im)

    f_re, f_im = local(pf_re, pf_im, bb_re[0], bb_im[0])
    g_re, g_im = local(pb_re, pb_im, bb_re[1], bb_im[1])
    w_state = jnp.concatenate([f_re, g_re, f_im, g_im], axis=-1)
    w1 = jnp.concatenate([w_intra, w_state], axis=-1)
    tf = 1 + jnp.arange(T)
    tb = T - jnp.arange(T)

    def readout(mr, mi):
        tr = lambda v: jnp.transpose(v, (1, 3, 0, 2)).reshape(G, P, T * GC)
        return tr(mr), tr(-mi)

    fr, fi = readout(m_re[tf, 0], m_im[tf, 0])
    br_, bi_ = readout(m_re[tb, 1], m_im[tb, 1])
    zp = jnp.zeros_like(fr)
    w2f = jnp.concatenate([fr, zp, fi, zp], axis=1)
    w2b = jnp.concatenate([zp, br_, zp, bi_], axis=1)
    a16 = jnp.concatenate([p_re[T, 0], p_re[T, 1], p_im[T, 0], p_im[T, 1]], axis=-1).reshape(G, 1, 4 * P)
    dsk = jnp.tile(d_skip.astype(F32).reshape(G, 1, GC), (1, 1, T))
    cast = lambda w: w.astype(MXU_DTYPE)
    return cast(w1), cast(w2f), cast(w2b), a16, dsk


def _s5_layer(xs, mod, g, params, glu_w, glu_b):
    B, S, D = xs.shape
    G, T, GC = D // S5_GROUP, S5_T, S5_GROUP
    nt = S // TS
    nchunks, nctx = S // T, TS // T
    main, _, _, mspec = _tile_specs(D, nt)
    h = pl.pallas_call(
        _s5_pre_kernel,
        grid=(B, nt),
        in_specs=[main, mspec, _const_spec((1, D))],
        out_specs=main,
        out_shape=jax.ShapeDtypeStruct(xs.shape, MXU_DTYPE),
        compiler_params=_cparams(("parallel", "parallel")),
        name="s5_norm",
    )(xs, mod, g.reshape(1, D))
    u = jnp.transpose(h.reshape(B, nchunks, T, G, GC), (3, 1, 0, 2, 4)).reshape(G, nchunks * B, T * GC)
    w1, w2f, w2b, a16, dsk = _s5_weights(*params)
    R, W = nchunks * B, T * GC
    gspec = lambda shape: pl.BlockSpec((1,) + shape, lambda gi: (gi, 0, 0))
    kern = functools.partial(_s5_core_kernel, nb=B, nctx=nctx, nchunks=nchunks, rb=16 * B)
    y = pl.pallas_call(
        kern,
        grid=(G,),
        in_specs=[gspec((R, W)), gspec((W, 2 * W)), gspec((W, W)), gspec((W, W)), gspec((1, W)), gspec((1, W))],
        out_specs=gspec((R, W)),
        out_shape=jax.ShapeDtypeStruct((G, R, W), MXU_DTYPE),
        scratch_shapes=[pltpu.VMEM((R, 2 * W), F32), pltpu.VMEM((R, W), F32), pltpu.VMEM((R, W), F32)],
        compiler_params=_cparams(("parallel",)),
        name="s5_core",
    )(u, w1, w2f, w2b, a16, dsk)
    yb = jnp.transpose(y.reshape(G, nchunks, B, T, GC), (2, 1, 3, 0, 4)).reshape(B, S, D)
    return pl.pallas_call(
        functools.partial(_s5_glu_kernel, D=D),
        grid=(B, nt),
        in_specs=[main, main, mspec, _const_spec((D, 2 * D)), _const_spec((1, 2 * D))],
        out_specs=main,
        out_shape=jax.ShapeDtypeStruct(xs.shape, F32),
        compiler_params=_cparams(("parallel", "parallel")),
        name="s5_glu",
    )(xs, yb, mod, glu_w.astype(MXU_DTYPE), glu_b.reshape(1, 2 * D))


def _ml_pre_kernel(x_ref, xp_ref, xn_ref, mod_ref, g_ref, win_ref, cw_ref, cos_ref, sin_ref,
                   wif_ref, bif_ref, wift_ref, bift_ref,
                   q_ref, k_ref, v_ref, o_ref, gc_ref, gr_ref, *, D, nt):
    t = pl.program_id(1)
    mod = mod_ref[0, 0]
    sh, sc = mod[0:1], mod[1:2]
    g = g_ref[...]
    h = _norm_mod(x_ref[0], g, sc, sh).astype(MXU_DTYPE)
    hh = _norm_mod(jnp.concatenate([xp_ref[0], xn_ref[0]], axis=0), g, sc, sh).astype(MXU_DTYPE)
    prev_ok, next_ok = _halo_valid(t, nt)
    dh = D // ML_HEADS
    cos, sin = cos_ref[...], sin_ref[...]
    for part, out_ref, scale in ((0, q_ref, dh ** -0.5), (1, k_ref, 1.0)):
        for hd in range(ML_HEADS):
            off = part * D + hd * dh
            w = win_ref[:, off:off + dh]
            u = _dot(h, w)
            uh = _dot(hh, w)
            cw = cw_ref[:, off:off + dh]
            dn, up = _shift_rows(u, uh[HALO - 1:HALO] * prev_ok, uh[HALO:HALO + 1] * next_ok)
            y = cw[0:1] * dn + cw[1:2] * u + cw[2:3] * up
            y = y * _sigmoid(y)
            sw = jnp.concatenate([pltpu.roll(y[:, i * 128:(i + 1) * 128], 64, axis=1) for i in range(dh // 128)],
                                 axis=1)
            y = y * cos + sw * sin
            out_ref[0, :, hd * dh:(hd + 1) * dh] = (y * scale).astype(out_ref.dtype)
    v_ref[0] = _dot(h, win_ref[:, 2 * D:3 * D]).astype(v_ref.dtype)
    o_ref[0] = _sigmoid(_dot(h, win_ref[:, 3 * D:4 * D])).astype(o_ref.dtype)
    gcol = _dot(h, wif_ref[...]) + bif_ref[...]
    lane = lax.broadcasted_iota(jnp.int32, gcol.shape, 1)
    gc_ref[0] = jnp.where((lane % 8) >= ML_HEADS, _log_sigmoid(gcol), gcol)
    grow = _dot_nt(wift_ref[...], h) + bift_ref[...]
    row = lax.broadcasted_iota(jnp.int32, grow.shape, 0)
    gr_ref[0] = jnp.where((row % 8) >= ML_HEADS, _log_sigmoid(grow), grow)


def _ml_scan_kernel(q_ref, k_ref, v_ref, gc_ref, gr_ref, o_ref, ct_ref, n_ref, m_ref, *, D):
    d = pl.program_id(1)
    j = pl.program_id(2)
    H = ML_HEADS
    dh = D // H
    T = q_ref.shape[1]

    @pl.when(j == 0)
    def _():
        ct_ref[...] = jnp.zeros_like(ct_ref)
        n_ref[...] = jnp.zeros_like(n_ref)
        m_ref[...] = jnp.zeros_like(m_ref)

    fwd = d == 0
    gc = gc_ref[0]
    gr = gr_ref[0]
    r_i = lax.broadcasted_iota(jnp.int32, (T, T), 0)
    c_i = lax.broadcasted_iota(jnp.int32, (T, T), 1)
    allowed = jnp.where(fwd, r_i, c_i) >= jnp.where(fwd, c_i, r_i)
    tri = allowed.astype(F32)
    hi = lax.Precision.HIGHEST
    b_cols = jnp.dot(tri, gc, precision=hi, preferred_element_type=F32)
    b_rows = lax.dot_general(gr, tri, (((1,), (1,)), ((), ())), precision=hi,
                             preferred_element_type=F32)
    for h in range(H):
        pick_c = lambda a, col: jnp.where(fwd, a[:, col:col + 1], a[:, 8 + col:9 + col])
        pick_r = lambda a, rw: jnp.where(fwd, a[rw:rw + 1], a[8 + rw:9 + rw])
        i_col, f_col, bc = pick_c(gc, h), pick_c(gc, H + h), pick_c(b_cols, H + h)
        i_row, br = pick_r(gr, h), pick_r(b_rows, H + h)
        m_prev = m_ref[h:h + 1, 0:1]
        log_d = jnp.where(allowed, bc - br + i_row, NEG)
        log_inter = bc + m_prev
        m_t = jnp.maximum(log_inter, jnp.max(log_d, axis=-1, keepdims=True))
        qh = q_ref[0, :, h * dh:(h + 1) * dh]
        kh = k_ref[0, :, h * dh:(h + 1) * dh]
        vh = v_ref[0, :, h * dh:(h + 1) * dh]
        w_intra = _dot_nt(qh, kh) * jnp.exp(log_d - m_t)
        w_inter = jnp.exp(log_inter - m_t)
        ct = ct_ref[h]
        nv = n_ref[h]
        num = _dot(w_intra, vh) + w_inter * _dot(qh, ct)
        den = (jnp.sum(w_intra, axis=-1, keepdims=True)
               + w_inter * jnp.sum(qh.astype(F32) * nv, axis=-1, keepdims=True))
        o_ref[0, 0, :, h * dh:(h + 1) * dh] = num / jnp.maximum(jnp.abs(den), jnp.exp(-m_t))
        b_end = jnp.sum(f_col, axis=0, keepdims=True)
        log_w = b_end - bc + i_col
        m_new = jnp.maximum(b_end + m_prev, jnp.max(log_w, axis=0, keepdims=True))
        w = jnp.exp(log_w - m_new)
        decay = jnp.exp(b_end + m_prev - m_new)
        ct_ref[h] = decay * ct + _dot_tn(kh, w * vh.astype(F32))
        n_ref[h] = decay * nv + jnp.sum(w * kh.astype(F32), axis=0, keepdims=True)
        m_ref[h:h + 1, :] = jnp.broadcast_to(m_new, (1, m_ref.shape[1]))


def _ml_post_kernel(x_ref, hf_ref, hb_ref, o_ref, mod_ref, ng_ref, w_ref, out_ref, *, D):
    gate = mod_ref[0, 0][2:3]
    dh = D // ML_HEADS
    hs = hf_ref[0, 0] + hb_ref[0, 0]
    parts = []
    for h in range(ML_HEADS):
        seg = hs[:, h * dh:(h + 1) * dh]
        parts.append(seg * lax.rsqrt(jnp.mean(seg * seg, axis=-1, keepdims=True) + EPS))
    hn = jnp.concatenate(parts, axis=1) * ng_ref[...]
    out_ref[0] = x_ref[0] + gate * _dot(o_ref[0].astype(F32) * hn, w_ref[...])


def _rope_tables(L, nctx, dh):
    half = dh // 2
    nf = half // 2
    t = jnp.arange(L, dtype=jnp.int32)
    inv = ROPE_BASE ** (-jnp.arange(nf, dtype=F32) / nf)

    def axis(pos):
        ang = pos.astype(F32)[:, None] * inv[None, :]
        c, s = jnp.cos(ang), jnp.sin(ang)
        return jnp.concatenate([c, c], axis=1), jnp.concatenate([-s, s], axis=1)

    cr, sr = axis(t // GRID_W)
    cc, sc = axis(t % GRID_W)
    cos = jnp.concatenate([cr, cc], axis=1)
    sin = jnp.concatenate([sr, sc], axis=1)
    cos = jnp.concatenate([jnp.ones((nctx, dh), F32), cos], axis=0)
    sin = jnp.concatenate([jnp.zeros((nctx, dh), F32), sin], axis=0)
    return cos, sin


def _ml_layer(xs, mod, g, w_in, conv_w, w_if, b_if, norm_g, w_out):
    B, S, D = xs.shape
    H = ML_HEADS
    dh = D // H
    nt = S // TS
    main, prev, nxt, mspec = _tile_specs(D, nt)
    cos, sin = _rope_tables(S - TS, TS, dh)
    wif = jnp.concatenate([w_if[0], w_if[1]], axis=1)
    bif = jnp.concatenate([b_if[0], b_if[1]], axis=0)
    wif_pad = jnp.zeros((D, 128), F32).at[:, :4 * H].set(wif).astype(MXU_DTYPE)
    bif_pad = jnp.zeros((1, 128), F32).at[0, :4 * H].set(bif)
    tabspec = pl.BlockSpec((TS, dh), lambda b, t: (t, 0))
    bf = lambda: jax.ShapeDtypeStruct(xs.shape, MXU_DTYPE)
    q, k, v, o, gc, gr = pl.pallas_call(
        functools.partial(_ml_pre_kernel, D=D, nt=nt),
        grid=(B, nt),
        in_specs=[main, prev, nxt, mspec, _const_spec((1, D)), _const_spec((D, 4 * D)), _const_spec((3, 2 * D)),
                  tabspec, tabspec, _const_spec((D, 128)), _const_spec((1, 128)),
                  _const_spec((4 * H, D)), _const_spec((4 * H, 1))],
        out_specs=[main, main, main, main,
                   pl.BlockSpec((1, TS, 128), lambda b, t: (b, t, 0)),
                   pl.BlockSpec((1, 4 * H, TS), lambda b, t: (b, 0, t))],
        out_shape=[bf(), bf(), bf(), bf(),
                   jax.ShapeDtypeStruct((B, S, 128), F32), jax.ShapeDtypeStruct((B, 4 * H, S), F32)],
        compiler_params=_cparams(("parallel", "parallel")),
        name="mlstm_pre",
    )(xs, xs, xs, mod, g.reshape(1, D), w_in.astype(MXU_DTYPE), conv_w, cos, sin,
      wif_pad, bif_pad, wif.T.astype(MXU_DTYPE), bif.reshape(4 * H, 1))

    def blk(d, j):
        return jnp.where(d == 0, j, jnp.where(j == 0, 0, nt - j))

    cspec = pl.BlockSpec((1, TS, D), lambda b, d, j: (b, blk(d, j), 0))
    hs = pl.pallas_call(
        functools.partial(_ml_scan_kernel, D=D),
        grid=(B, 2, nt),
        in_specs=[cspec, cspec, cspec,
                  pl.BlockSpec((1, TS, 128), lambda b, d, j: (b, blk(d, j), 0)),
                  pl.BlockSpec((1, 4 * H, TS), lambda b, d, j: (b, 0, blk(d, j)))],
        out_specs=pl.BlockSpec((1, 1, TS, D), lambda b, d, j: (d, b, blk(d, j), 0)),
        out_shape=jax.ShapeDtypeStruct((2, B, S, D), F32),
        scratch_shapes=[pltpu.VMEM((H, dh, dh), F32), pltpu.VMEM((H, 1, dh), F32), pltpu.VMEM((8, 128), F32)],
        compiler_params=_cparams(("parallel", "arbitrary", "arbitrary")),
        name="mlstm_scan",
    )(q, k, v, gc, gr)
    dspec = lambda dd: pl.BlockSpec((1, 1, TS, D), lambda b, t: (dd, b, t, 0))
    return pl.pallas_call(
        functools.partial(_ml_post_kernel, D=D),
        grid=(B, nt),
        in_specs=[main, dspec(0), dspec(1), main, mspec, _const_spec((1, D)), _const_spec((D, D))],
        out_specs=main,
        out_shape=jax.ShapeDtypeStruct(xs.shape, F32),
        compiler_params=_cparams(("parallel", "parallel")),
        name="mlstm_post",
    )(xs, hs, hs, o, mod, norm_g.reshape(1, D), w_out.astype(MXU_DTYPE))


def _na_pre_kernel(x_ref, mod_ref, g_ref, w_ref, q_ref, k_ref, v_ref, *, D):
    mod = mod_ref[0, 0]
    h = _norm_mod(x_ref[0], g_ref[...], mod[1:2], mod[0:1]).astype(MXU_DTYPE)
    scale = (D // NA_HEADS) ** -0.5
    q_ref[0] = (_dot(h, w_ref[:, 0:D]) * scale).astype(q_ref.dtype)
    k_ref[0] = _dot(h, w_ref[:, D:2 * D]).astype(k_ref.dtype)
    v_ref[0] = _dot(h, w_ref[:, 2 * D:3 * D]).astype(v_ref.dtype)


def _na_attn_kernel(q_ref, k_ref, v_ref, bias_ref, o_ref, *, nctx, rows):
    t = pl.program_id(1)
    W = GRID_W
    nloc = NA_WIN_ROWS * W
    r = jnp.maximum(t - nctx // W, 0)
    r0 = jnp.clip(r - NA_WIN_ROWS // 2, 0, rows - NA_WIN_ROWS)
    base = pl.multiple_of(nctx + r0 * W, W)
    lane = lax.broadcasted_iota(jnp.int32, (W, 128), 1)
    for p in range(NA_HEADS // 2):
        sl = slice(p * 128, (p + 1) * 128)
        qp = q_ref[0, :, sl]
        kc, vc = k_ref[0, 0:nctx, sl], v_ref[0, 0:nctx, sl]
        kl, vl = k_ref[0, pl.ds(base, nloc), sl], v_ref[0, pl.ds(base, nloc), sl]
        outs = []
        for e in range(2):
            qm = jnp.where((lane >= 64) if e else (lane < 64), qp, jnp.zeros_like(qp))
            s_loc = _dot_nt(qm, kl) + bias_ref[0, 2 * p + e]
            s_ctx = _dot_nt(qm, kc)
            m = jnp.maximum(jnp.max(s_loc, axis=-1, keepdims=True), jnp.max(s_ctx, axis=-1, keepdims=True))
            p_loc, p_ctx = jnp.exp(s_loc - m), jnp.exp(s_ctx - m)
            den = jnp.sum(p_loc, axis=-1, keepdims=True) + jnp.sum(p_ctx, axis=-1, keepdims=True)
            outs.append((_dot(p_loc, vl) + _dot(p_ctx, vc)) / den)
        o_ref[0, :, sl] = jnp.where(lane < 64, outs[0], outs[1]).astype(o_ref.dtype)


def _proj_residual_kernel(x_ref, a_ref, mod_ref, w_ref, o_ref):
    gate = mod_ref[0, 0][2:3]
    o_ref[0] = x_ref[0] + gate * _dot(a_ref[0], w_ref[...])


def _na_bias_tables(rpb):
    H = rpb.shape[0]
    W, KR, KC = GRID_W, NA_WIN_ROWS, NA_WIN_COLS
    cols = jnp.arange(W, dtype=jnp.int32)
    c0 = jnp.clip(cols - KC // 2, 0, W - KC)
    col_mask = (cols[None, :] >= c0[:, None]) & (cols[None, :] < c0[:, None] + KC)
    dc_idx = jnp.clip(cols[None, :] - cols[:, None], 1 - KC, KC - 1) + KC - 1
    off = jnp.arange(KR, dtype=jnp.int32)[:, None]
    dr_idx = jnp.arange(KR, dtype=jnp.int32)[None, :] - off + KR - 1
    tab = rpb.astype(F32)[:, dr_idx[:, None, :, None], dc_idx[None, :, None, :]]
    tab = jnp.where(col_mask[None, None, :, None, :], tab, NEG)
    tab = jnp.transpose(tab, (1, 0, 2, 3, 4)).reshape(KR, H, W, KR * W)
    return jnp.concatenate([tab, jnp.full((1, H, W, KR * W), NEG, F32)], axis=0)


def _na_layer(xs, mod, g, w_qkv, rpb, w_out):
    B, S, D = xs.shape
    nt = S // TS
    W = GRID_W
    rows = (S - TS) // W
    assert rows >= NA_WIN_ROWS
    main, _, _, mspec = _tile_specs(D, nt)
    bf = lambda: jax.ShapeDtypeStruct(xs.shape, MXU_DTYPE)
    q, k, v = pl.pallas_call(
        functools.partial(_na_pre_kernel, D=D),
        grid=(B, nt),
        in_specs=[main, mspec, _const_spec((1, D)), _const_spec((D, 3 * D))],
        out_specs=[main, main, main],
        out_shape=[bf(), bf(), bf()],
        compiler_params=_cparams(("parallel", "parallel")),
        name="na_qkv",
    )(xs, mod, g.reshape(1, D), w_qkv.astype(MXU_DTYPE))
    bias = _na_bias_tables(rpb)
    nq = TS // W

    def variant(t):
        r = t - nq
        r0 = jnp.clip(r - NA_WIN_ROWS // 2, 0, rows - NA_WIN_ROWS)
        return jnp.where(t < nq, NA_WIN_ROWS, r - r0)

    qspec = pl.BlockSpec((1, W, D), lambda b, t: (b, t, 0))
    full = pl.BlockSpec((1, S, D), lambda b, t: (b, 0, 0))
    o = pl.pallas_call(
        functools.partial(_na_attn_kernel, nctx=TS, rows=rows),
        grid=(B, S // W),
        in_specs=[qspec, full, full,
                  pl.BlockSpec((1, NA_HEADS, W, NA_WIN_ROWS * W), lambda b, t: (variant(t), 0, 0, 0))],
        out_specs=qspec,
        out_shape=bf(),
        compiler_params=_cparams(("parallel", "arbitrary")),
        name="na_attention",
    )(q, k, v, bias)
    return pl.pallas_call(
        _proj_residual_kernel,
        grid=(B, nt),
        in_specs=[main, main, mspec, _const_spec((D, D))],
        out_specs=main,
        out_shape=jax.ShapeDtypeStruct(xs.shape, F32),
        compiler_params=_cparams(("parallel", "parallel")),
        name="na_out",
    )(xs, o, mod, w_out.astype(MXU_DTYPE))


def _final_norm_kernel(x_ref, g_ref, o_ref):
    x = x_ref[0]
    o_ref[0] = x * lax.rsqrt(jnp.mean(x * x, axis=-1, keepdims=True) + EPS) * g_ref[...]


def _final_norm(xs, g):
    B, S, D = xs.shape
    nl = S // TS - 1
    return pl.pallas_call(
        _final_norm_kernel,
        grid=(B, nl),
        in_specs=[pl.BlockSpec((1, TS, D), lambda b, t: (b, t + 1, 0)), _const_spec((1, D))],
        out_specs=pl.BlockSpec((1, TS, D), lambda b, t: (b, t, 0)),
        out_shape=jax.ShapeDtypeStruct((B, S - TS, D), F32),
        compiler_params=_cparams(("parallel", "parallel")),
        name="final_norm",
    )(xs, g.reshape(1, D))


def kernel(x, c, ctx, c_ctx, ada_w, ada_b, norm1_g, norm2_g, ffn_w_in, ffn_conv, ffn_w_out, s5_lam_re, s5_lam_im, s5_log_dt, s5_b_re, s5_b_im, s5_c_re, s5_c_im, s5_d, s5_glu_w, s5_glu_b, ml_w_in, ml_conv, ml_w_if, ml_b_if, ml_norm_g, ml_w_out, na_w_qkv, na_rpb, na_w_out, final_g):
    assert ctx.shape[1] == TS and x.shape[1] % TS == 0 and x.shape[0] % 8 == 0
    depth = ada_w.shape[0]
    xs = jnp.concatenate([ctx, x], axis=1)
    mods = _modulation_all(c, c_ctx, ada_w, ada_b)
    for i in range(depth):
        kind, j = i % 3, i // 3
        if kind == 0:
            params = (s5_lam_re[j], s5_lam_im[j], s5_log_dt[j], s5_b_re[j], s5_b_im[j], s5_c_re[j], s5_c_im[j],
                      s5_d[j])
            xs = _s5_layer(xs, mods[i], norm1_g[i], params, s5_glu_w[j], s5_glu_b[j])
        elif kind == 1:
            xs = _ml_layer(xs, mods[i], norm1_g[i], ml_w_in[j], ml_conv[j], ml_w_if[j], ml_b_if[j],
                           ml_norm_g[j], ml_w_out[j])
        else:
            xs = _na_layer(xs, mods[i], norm1_g[i], na_w_qkv[j], na_rpb[j], na_w_out[j])
        xs = _ffn_layer(xs, mods[i], norm2_g[i], ffn_w_in[i], ffn_conv[i], ffn_w_out[i])
    return _final_norm(xs, final_g)
```

```python
import functools
import math

import jax
import jax.numpy as jnp
from jax import lax
from jax.experimental import pallas as pl
from jax.experimental.pallas import tpu as pltpu

F32 = jnp.float32
MXU_DTYPE = jnp.bfloat16
EPS = 1e-6
NEG = -1e30
TS = 256
HALO = 8
GRID_W = 64
ROPE_BASE = 10000.0
S5_GROUP = 16
S5_STATE = 64
S5_T = 16
ML_HEADS = 4
NA_HEADS = 16
NA_WIN_ROWS = 8
NA_WIN_COLS = 16
VMEM_LIMIT = 56 * 1024 * 1024


def _cparams(sem):
    return pltpu.CompilerParams(dimension_semantics=sem, vmem_limit_bytes=VMEM_LIMIT)


def _dot(a, b):
    return jnp.dot(a.astype(MXU_DTYPE), b.astype(MXU_DTYPE), preferred_element_type=F32)


def _dot_nt(a, b):
    return lax.dot_general(a.astype(MXU_DTYPE), b.astype(MXU_DTYPE), (((1,), (1,)), ((), ())),
                           preferred_element_type=F32)


def _dot_tn(a, b):
    return lax.dot_general(a.astype(MXU_DTYPE), b.astype(MXU_DTYPE), (((0,), (0,)), ((), ())),
                           preferred_element_type=F32)


def _norm_mod(x, g, sc, sh):
    ms = jnp.mean(x * x, axis=-1, keepdims=True)
    return (x * lax.rsqrt(ms + EPS) * g) * (1.0 + sc) + sh


def _sigmoid(x):
    return 1.0 / (1.0 + jnp.exp(-x))


def _log_sigmoid(x):
    return jnp.minimum(x, 0.0) - jnp.log1p(jnp.exp(-jnp.abs(x)))


def _ext_rows_normed(x, xp, xn, g, sc, sh, t, nt):
    n = x.shape[0]
    xe = jnp.concatenate([xp, x, xn], axis=0)
    rows = lax.broadcasted_iota(jnp.int32, (n + 2 * HALO, 1), 0)
    prev_ok = t >= 2
    next_ok = jnp.logical_and(t >= 1, t <= nt - 2)
    keep = jnp.logical_and(jnp.logical_or(rows >= HALO, prev_ok), jnp.logical_or(rows < HALO + n, next_ok))
    return _norm_mod(xe, g, sc, sh) * keep.astype(F32)


def _conv3_rows(u_ref, c, cw):
    n = u_ref.shape[1] - 2 * HALO
    return (cw[0:1] * u_ref[c, HALO - 1:HALO - 1 + n, :] + cw[1:2] * u_ref[c, HALO:HALO + n, :]
            + cw[2:3] * u_ref[c, HALO + 1:HALO + 1 + n, :])


def _tile_specs(D, nt):
    per = TS // HALO
    main = pl.BlockSpec((1, TS, D), lambda b, t: (b, t, 0))
    prev = pl.BlockSpec((1, HALO, D), lambda b, t: (b, jnp.maximum(t * per - 1, 0), 0))
    nxt = pl.BlockSpec((1, HALO, D), lambda b, t: (b, jnp.minimum((t + 1) * per, nt * per - 1), 0))
    mod = pl.BlockSpec((1, 1, 6, D), lambda b, t: (b, jnp.minimum(t, 1), 0, 0))
    return main, prev, nxt, mod


def _const_spec(shape):
    nd = len(shape)
    return pl.BlockSpec(shape, lambda *_: (0,) * nd)


def _mod_kernel(c_ref, w_ref, b_ref, o_ref):
    cv = c_ref[...]
    o_ref[0] = _dot(cv * _sigmoid(cv), w_ref[0]) + b_ref[0]


def _modulation_all(c, c_ctx, ada_w, ada_b):
    depth, D, D6 = ada_w.shape
    B = c.shape[0]
    rows = ((B + 1 + 7) // 8) * 8
    cc = jnp.zeros((rows, D), F32).at[:B].set(c).at[B].set(c_ctx)
    nb = 4
    out = pl.pallas_call(
        _mod_kernel,
        grid=(depth, nb),
        in_specs=[pl.BlockSpec((rows, D), lambda i, n: (0, 0)),
                  pl.BlockSpec((1, D, D6 // nb), lambda i, n: (i, 0, n)),
                  pl.BlockSpec((1, 1, D6 // nb), lambda i, n: (i, 0, n))],
        out_specs=pl.BlockSpec((1, rows, D6 // nb), lambda i, n: (i, 0, n)),
        out_shape=jax.ShapeDtypeStruct((depth, rows, D6), F32),
        compiler_params=_cparams(("parallel", "parallel")),
        name="adaln_modulation",
    )(cc, ada_w, ada_b.reshape(depth, 1, D6))
    lat = out[:, :B].reshape(depth, B, 1, 6, D)
    cx = jnp.broadcast_to(out[:, B].reshape(depth, 1, 1, 6, D), (depth, B, 1, 6, D))
    return jnp.concatenate([cx, lat], axis=2)


def _ffn_kernel(x_ref, xp_ref, xn_ref, mod_ref, g_ref, win_ref, cw_ref, wout_ref, o_ref,
                ua_ref, ug_ref, act_ref, *, F, FC, nt):
    t = pl.program_id(1)
    mod = mod_ref[0, 0]
    sh, sc, gate = mod[3:4], mod[4:5], mod[5:6]
    x = x_ref[0]
    h = _ext_rows_normed(x, xp_ref[0], xn_ref[0], g_ref[...], sc, sh, t, nt).astype(MXU_DTYPE)
    for c in range(F // FC):
        ua_ref[c] = _dot(h, win_ref[:, c * FC:(c + 1) * FC])
        ug_ref[c] = _dot(h, win_ref[:, F + c * FC:F + (c + 1) * FC])
        a = _conv3_rows(ua_ref, c, cw_ref[:, c * FC:(c + 1) * FC])
        gg = _conv3_rows(ug_ref, c, cw_ref[:, F + c * FC:F + (c + 1) * FC])
        act_ref[:, c * FC:(c + 1) * FC] = (a * (gg * _sigmoid(gg))).astype(act_ref.dtype)
    o_ref[0] = x + gate * _dot(act_ref[...], wout_ref[...])


def _ffn_layer(xs, mod, g, w_in, conv_w, w_out):
    B, S, D = xs.shape
    F = w_out.shape[0]
    FC = 256
    nt = S // TS
    main, prev, nxt, mspec = _tile_specs(D, nt)
    kern = functools.partial(_ffn_kernel, F=F, FC=FC, nt=nt)
    ext = pltpu.VMEM((F // FC, TS + 2 * HALO, FC), F32)
    return pl.pallas_call(
        kern,
        grid=(B, nt),
        in_specs=[main, prev, nxt, mspec, _const_spec((1, D)), _const_spec((D, 2 * F)),
                  _const_spec((3, 2 * F)), _const_spec((F, D))],
        out_specs=main,
        out_shape=jax.ShapeDtypeStruct(xs.shape, F32),
        scratch_shapes=[ext, ext, pltpu.VMEM((TS, F), MXU_DTYPE)],
        compiler_params=_cparams(("parallel", "parallel")),
        name="conv_ffn",
    )(xs, xs, xs, mod, g.reshape(1, D), w_in.astype(MXU_DTYPE), conv_w, w_out.astype(MXU_DTYPE))


def _s5_pre_kernel(x_ref, mod_ref, g_ref, o_ref):
    mod = mod_ref[0, 0]
    o_ref[0] = _norm_mod(x_ref[0], g_ref[...], mod[1:2], mod[0:1]).astype(o_ref.dtype)


def _s5_core_kernel(u_ref, w1_ref, w2f_ref, w2b_ref, a16_ref, d_ref, o_ref, z_ref, xpf_ref, xpb_ref,
                    *, nb, nctx, nchunks, rb):
    R = nchunks * nb
    w1 = w1_ref[0]

    def proj(i, carry):
        r0 = pl.multiple_of(i * rb, rb)
        z_ref[pl.ds(r0, rb), :] = _dot(u_ref[0, pl.ds(r0, rb), :], w1)
        return carry

    lax.fori_loop(0, R // rb, proj, 0)

    a16 = a16_ref[0]
    ar, ai = a16[:, :128], a16[:, 128:]
    is_f = lax.broadcasted_iota(jnp.int32, (nb, 128), 1) < S5_STATE

    def step(k, carry):
        xr, xi = carry
        cb = jnp.where(k < nctx, nctx - 1 - k, nchunks - 1 - (k - nctx))
        rf = pl.multiple_of(k * nb, nb)
        rbk = pl.multiple_of(cb * nb, nb)
        st = jnp.concatenate([xr, xi], axis=1)
        xpf_ref[pl.ds(rf, nb), :] = st
        xpb_ref[pl.ds(rbk, nb), :] = st
        ef = z_ref[pl.ds(rf, nb), 256:512]
        eb = z_ref[pl.ds(rbk, nb), 256:512]
        er = jnp.where(is_f, ef[:, :128], eb[:, :128])
        ei = jnp.where(is_f, ef[:, 128:], eb[:, 128:])
        return ar * xr - ai * xi + er, ar * xi + ai * xr + ei

    zero = jnp.zeros((nb, 128), F32)
    lax.fori_loop(0, nchunks, step, (zero, zero))

    w2f, w2b, dsk = w2f_ref[0], w2b_ref[0], d_ref[0]

    def finish(i, carry):
        r0 = pl.multiple_of(i * rb, rb)
        rows = pl.ds(r0, rb)
        y = z_ref[rows, 0:256] + _dot(xpf_ref[rows, :], w2f) + _dot(xpb_ref[rows, :], w2b)
        y = y + dsk * u_ref[0, rows, :].astype(F32)
        o_ref[0, rows, :] = jax.nn.gelu(y).astype(o_ref.dtype)
        return carry

    lax.fori_loop(0, R // rb, finish, 0)


def _s5_glu_kernel(x_ref, y_ref, mod_ref, w_ref, b_ref, o_ref, *, D):
    gate = mod_ref[0, 0][2:3]
    z = _dot(y_ref[0], w_ref[...]) + b_ref[...]
    o_ref[0] = x_ref[0] + gate * (z[:, :D] * _sigmoid(z[:, D:]))


def _s5_weights(lam_re, lam_im, log_dt, b_re, b_im, c_re, c_im, d_skip):
    T, P, GC = S5_T, S5_STATE, S5_GROUP
    G = lam_re.shape[1]
    lam_re, lam_im = lam_re.astype(F32), lam_im.astype(F32)
    dt = jnp.exp(log_dt.astype(F32))[..., None]
    lr = jnp.minimum(lam_re, -1e-4)
    er = jnp.exp(lr * dt)
    ang = lam_im * dt
    a_re, a_im = er * jnp.cos(ang), er * jnp.sin(ang)
    den = lr * lr + lam_im * lam_im
    nr = a_re - 1.0
    q_re = (nr * lr + a_im * lam_im) / den
    q_im = (a_im * lr - nr * lam_im) / den
    bb_re = q_re[..., None] * b_re - q_im[..., None] * b_im
    bb_im = q_re[..., None] * b_im + q_im[..., None] * b_re
    tau = jnp.arange(T + 1, dtype=F32)[:, None, None, None]
    mag = jnp.exp(lr * dt * tau)
    p_re, p_im = mag * jnp.cos(ang * tau), mag * jnp.sin(ang * tau)
    c_re, c_im = c_re.astype(F32), c_im.astype(F32)
    m_re = c_re[None] * p_re[:, :, :, None, :] - c_im[None] * p_im[:, :, :, None, :]
    m_im = c_re[None] * p_im[:, :, :, None, :] + c_im[None] * p_re[:, :, :, None, :]
    hi = lax.Precision.HIGHEST
    kern = (jnp.einsum('tdghp,dgpi->tdghi', m_re[:T], bb_re, precision=hi)
            - jnp.einsum('tdghp,dgpi->tdghi', m_im[:T], bb_im, precision=hi))
    s_i = jnp.arange(T)[:, None]
    t_i = jnp.arange(T)[None, :]
    kf = kern[jnp.clip(t_i - s_i, 0, T - 1), 0]
    kb = kern[jnp.clip(s_i - t_i, 0, T - 1), 1]
    kf = jnp.where((t_i >= s_i)[:, :, None, None, None], kf, 0.0)
    kb = jnp.where((s_i >= t_i)[:, :, None, None, None], kb, 0.0)
    w_intra = jnp.transpose(kf + kb, (2, 0, 4, 1, 3)).reshape(G, T * GC, T * GC)
    pf_re, pf_im = p_re[T - 1 - jnp.arange(T), 0], p_im[T - 1 - jnp.arange(T), 0]
    pb_re, pb_im = p_re[:T, 1], p_im[:T, 1]

    def local(pr, pi, br, bi):
        re = pr[..., None] * br[None] - pi[..., None] * bi[None]
        im = pr[..., None] * bi[None] + pi[..., None] * br[None]
        tr = lambda v: jnp.transpose(v, (1, 0, 3, 2)).reshape(G, T * GC, P)
        return tr(re), tr(im)

    f_re, f_im = local(pf_re, pf_im, bb_re[0], bb_im[0])
    g_re, g_im = local(pb_re, pb_im, bb_re[1], bb_im[1])
    w_state = jnp.concatenate([f_re, g_re, f_im, g_im], axis=-1)
    w1 = jnp.concatenate([w_intra, w_state], axis=-1)
    tf = 1 + jnp.arange(T)
    tb = T - jnp.arange(T)

    def readout(mr, mi):
        tr = lambda v: jnp.transpose(v, (1, 3, 0, 2)).reshape(G, P, T * GC)
        return tr(mr), tr(-mi)

    fr, fi = readout(m_re[tf, 0], m_im[tf, 0])
    br_, bi_ = readout(m_re[tb, 1], m_im[tb, 1])
    zp = jnp.zeros_like(fr)
    w2f = jnp.concatenate([fr, zp, fi, zp], axis=1)
    w2b = jnp.concatenate([zp, br_, zp, bi_], axis=1)
    a16 = jnp.concatenate([p_re[T, 0], p_re[T, 1], p_im[T, 0], p_im[T, 1]], axis=-1).reshape(G, 1, 4 * P)
    dsk = jnp.tile(d_skip.astype(F32).reshape(G, 1, GC), (1, 1, T))
    cast = lambda w: w.astype(MXU_DTYPE)
    return cast(w1), cast(w2f), cast(w2b), a16, dsk


def _s5_layer(xs, mod, g, params, glu_w, glu_b):
    B, S, D = xs.shape
    G, T, GC = D // S5_GROUP, S5_T, S5_GROUP
    nt = S // TS
    nchunks, nctx = S // T, TS // T
    main, _, _, mspec = _tile_specs(D, nt)
    h = pl.pallas_call(
        _s5_pre_kernel,
        grid=(B, nt),
        in_specs=[main, mspec, _const_spec((1, D))],
        out_specs=main,
        out_shape=jax.ShapeDtypeStruct(xs.shape, MXU_DTYPE),
        compiler_params=_cparams(("parallel", "parallel")),
        name="s5_norm",
    )(xs, mod, g.reshape(1, D))
    u = jnp.transpose(h.reshape(B, nchunks, T, G, GC), (3, 1, 0, 2, 4)).reshape(G, nchunks * B, T * GC)
    w1, w2f, w2b, a16, dsk = _s5_weights(*params)
    R, W = nchunks * B, T * GC
    gspec = lambda shape: pl.BlockSpec((1,) + shape, lambda gi: (gi, 0, 0))
    kern = functools.partial(_s5_core_kernel, nb=B, nctx=nctx, nchunks=nchunks, rb=16 * B)
    y = pl.pallas_call(
        kern,
        grid=(G,),
        in_specs=[gspec((R, W)), gspec((W, 2 * W)), gspec((W, W)), gspec((W, W)), gspec((1, W)), gspec((1, W))],
        out_specs=gspec((R, W)),
        out_shape=jax.ShapeDtypeStruct((G, R, W), MXU_DTYPE),
        scratch_shapes=[pltpu.VMEM((R, 2 * W), F32), pltpu.VMEM((R, W), F32), pltpu.VMEM((R, W), F32)],
        compiler_params=_cparams(("parallel",)),
        name="s5_core",
    )(u, w1, w2f, w2b, a16, dsk)
    yb = jnp.transpose(y.reshape(G, nchunks, B, T, GC), (2, 1, 3, 0, 4)).reshape(B, S, D)
    return pl.pallas_call(
        functools.partial(_s5_glu_kernel, D=D),
        grid=(B, nt),
        in_specs=[main, main, mspec, _const_spec((D, 2 * D)), _const_spec((1, 2 * D))],
        out_specs=main,
        out_shape=jax.ShapeDtypeStruct(xs.shape, F32),
        compiler_params=_cparams(("parallel", "parallel")),
        name="s5_glu",
    )(xs, yb, mod, glu_w.astype(MXU_DTYPE), glu_b.reshape(1, 2 * D))


def _ml_pre_kernel(x_ref, xp_ref, xn_ref, mod_ref, g_ref, win_ref, cw_ref, cos_ref, sin_ref,
                   wif_ref, bif_ref, wift_ref, bift_ref,
                   q_ref, k_ref, v_ref, o_ref, gc_ref, gr_ref, u_ref, *, D, nt):
    t = pl.program_id(1)
    mod = mod_ref[0, 0]
    sh, sc = mod[0:1], mod[1:2]
    he = _ext_rows_normed(x_ref[0], xp_ref[0], xn_ref[0], g_ref[...], sc, sh, t, nt)
    h = he[HALO:HALO + TS].astype(MXU_DTYPE)
    he = he.astype(MXU_DTYPE)
    dh = D // ML_HEADS
    cos, sin = cos_ref[...], sin_ref[...]
    for part, out_ref, scale in ((0, q_ref, dh ** -0.5), (1, k_ref, 1.0)):
        for hd in range(ML_HEADS):
            off = part * D + hd * dh
            c = part * ML_HEADS + hd
            u_ref[c] = _dot(he, win_ref[:, off:off + dh])
            y = _conv3_rows(u_ref, c, cw_ref[:, off:off + dh])
            y = y * _sigmoid(y)
            sw = jnp.concatenate([pltpu.roll(y[:, i * 128:(i + 1) * 128], 64, axis=1) for i in range(dh // 128)],
                                 axis=1)
            y = y * cos + sw * sin
            out_ref[0, :, hd * dh:(hd + 1) * dh] = (y * scale).astype(out_ref.dtype)
    v_ref[0] = _dot(h, win_ref[:, 2 * D:3 * D]).astype(v_ref.dtype)
    o_ref[0] = _sigmoid(_dot(h, win_ref[:, 3 * D:4 * D])).astype(o_ref.dtype)
    gcol = _dot(h, wif_ref[...]) + bif_ref[...]
    lane = lax.broadcasted_iota(jnp.int32, gcol.shape, 1)
    gc_ref[0] = jnp.where((lane % 8) >= ML_HEADS, _log_sigmoid(gcol), gcol)
    grow = _dot_nt(wift_ref[...], h) + bift_ref[...]
    row = lax.broadcasted_iota(jnp.int32, grow.shape, 0)
    gr_ref[0] = jnp.where((row % 8) >= ML_HEADS, _log_sigmoid(grow), grow)


def _ml_scan_kernel(q_ref, k_ref, v_ref, gc_ref, gr_ref, o_ref, ct_ref, n_ref, m_ref, *, D):
    d = pl.program_id(1)
    j = pl.program_id(2)
    H = ML_HEADS
    dh = D // H
    T = q_ref.shape[1]

    @pl.when(j == 0)
    def _():
        ct_ref[...] = jnp.zeros_like(ct_ref)
        n_ref[...] = jnp.zeros_like(n_ref)
        m_ref[...] = jnp.zeros_like(m_ref)

    fwd = d == 0
    gc = gc_ref[0]
    gr = gr_ref[0]
    r_i = lax.broadcasted_iota(jnp.int32, (T, T), 0)
    c_i = lax.broadcasted_iota(jnp.int32, (T, T), 1)
    allowed = jnp.where(fwd, r_i, c_i) >= jnp.where(fwd, c_i, r_i)
    tri = allowed.astype(F32)
    hi = lax.Precision.HIGHEST
    b_cols = jnp.dot(tri, gc, precision=hi, preferred_element_type=F32)
    b_rows = lax.dot_general(gr, tri, (((1,), (1,)), ((), ())), precision=hi,
                             preferred_element_type=F32)
    for h in range(H):
        pick_c = lambda a, col: jnp.where(fwd, a[:, col:col + 1], a[:, 8 + col:9 + col])
        pick_r = lambda a, rw: jnp.where(fwd, a[rw:rw + 1], a[8 + rw:9 + rw])
        i_col, f_col, bc = pick_c(gc, h), pick_c(gc, H + h), pick_c(b_cols, H + h)
        i_row, br = pick_r(gr, h), pick_r(b_rows, H + h)
        m_prev = m_ref[h:h + 1, 0:1]
        log_d = jnp.where(allowed, bc - br + i_row, NEG)
        log_inter = bc + m_prev
        m_t = jnp.maximum(log_inter, jnp.max(log_d, axis=-1, keepdims=True))
        qh = q_ref[0, :, h * dh:(h + 1) * dh]
        kh = k_ref[0, :, h * dh:(h + 1) * dh]
        vh = v_ref[0, :, h * dh:(h + 1) * dh]
        w_intra = _dot_nt(qh, kh) * jnp.exp(log_d - m_t)
        w_inter = jnp.exp(log_inter - m_t)
        ct = ct_ref[h]
        nv = n_ref[h]
        num = _dot(w_intra, vh) + w_inter * _dot(qh, ct)
        den = (jnp.sum(w_intra, axis=-1, keepdims=True)
               + w_inter * jnp.sum(qh.astype(F32) * nv, axis=-1, keepdims=True))
        o_ref[0, 0, :, h * dh:(h + 1) * dh] = num / jnp.maximum(jnp.abs(den), jnp.exp(-m_t))
        b_end = jnp.sum(f_col, axis=0, keepdims=True)
        log_w = b_end - bc + i_col
        m_new = jnp.maximum(b_end + m_prev, jnp.max(log_w, axis=0, keepdims=True))
        w = jnp.exp(log_w - m_new)
        decay = jnp.exp(b_end + m_prev - m_new)
        ct_ref[h] = decay * ct + _dot_tn(kh, w * vh.astype(F32))
        n_ref[h] = decay * nv + jnp.sum(w * kh.astype(F32), axis=0, keepdims=True)
        m_ref[h:h + 1, :] = jnp.broadcast_to(m_new, (1, m_ref.shape[1]))


def _ml_post_kernel(x_ref, hf_ref, hb_ref, o_ref, mod_ref, ng_ref, w_ref, out_ref, *, D):
    gate = mod_ref[0, 0][2:3]
    dh = D // ML_HEADS
    hs = hf_ref[0, 0] + hb_ref[0, 0]
    parts = []
    for h in range(ML_HEADS):
        seg = hs[:, h * dh:(h + 1) * dh]
        parts.append(seg * lax.rsqrt(jnp.mean(seg * seg, axis=-1, keepdims=True) + EPS))
    hn = jnp.concatenate(parts, axis=1) * ng_ref[...]
    out_ref[0] = x_ref[0] + gate * _dot(o_ref[0].astype(F32) * hn, w_ref[...])


def _rope_tables(L, nctx, dh):
    half = dh // 2
    nf = half // 2
    t = jnp.arange(L, dtype=jnp.int32)
    inv = ROPE_BASE ** (-jnp.arange(nf, dtype=F32) / nf)

    def axis(pos):
        ang = pos.astype(F32)[:, None] * inv[None, :]
        c, s = jnp.cos(ang), jnp.sin(ang)
        return jnp.concatenate([c, c], axis=1), jnp.concatenate([-s, s], axis=1)

    cr, sr = axis(t // GRID_W)
    cc, sc = axis(t % GRID_W)
    cos = jnp.concatenate([cr, cc], axis=1)
    sin = jnp.concatenate([sr, sc], axis=1)
    cos = jnp.concatenate([jnp.ones((nctx, dh), F32), cos], axis=0)
    sin = jnp.concatenate([jnp.zeros((nctx, dh), F32), sin], axis=0)
    return cos, sin


def _ml_layer(xs, mod, g, w_in, conv_w, w_if, b_if, norm_g, w_out):
    B, S, D = xs.shape
    H = ML_HEADS
    dh = D // H
    nt = S // TS
    main, prev, nxt, mspec = _tile_specs(D, nt)
    cos, sin = _rope_tables(S - TS, TS, dh)
    wif = jnp.concatenate([w_if[0], w_if[1]], axis=1)
    bif = jnp.concatenate([b_if[0], b_if[1]], axis=0)
    wif_pad = jnp.zeros((D, 128), F32).at[:, :4 * H].set(wif).astype(MXU_DTYPE)
    bif_pad = jnp.zeros((1, 128), F32).at[0, :4 * H].set(bif)
    tabspec = pl.BlockSpec((TS, dh), lambda b, t: (t, 0))
    bf = lambda: jax.ShapeDtypeStruct(xs.shape, MXU_DTYPE)
    q, k, v, o, gc, gr = pl.pallas_call(
        functools.partial(_ml_pre_kernel, D=D, nt=nt),
        grid=(B, nt),
        in_specs=[main, prev, nxt, mspec, _const_spec((1, D)), _const_spec((D, 4 * D)), _const_spec((3, 2 * D)),
                  tabspec, tabspec, _const_spec((D, 128)), _const_spec((1, 128)),
                  _const_spec((4 * H, D)), _const_spec((4 * H, 1))],
        out_specs=[main, main, main, main,
                   pl.BlockSpec((1, TS, 128), lambda b, t: (b, t, 0)),
                   pl.BlockSpec((1, 4 * H, TS), lambda b, t: (b, 0, t))],
        out_shape=[bf(), bf(), bf(), bf(),
                   jax.ShapeDtypeStruct((B, S, 128), F32), jax.ShapeDtypeStruct((B, 4 * H, S), F32)],
        scratch_shapes=[pltpu.VMEM((2 * H, TS + 2 * HALO, dh), F32)],
        compiler_params=_cparams(("parallel", "parallel")),
        name="mlstm_pre",
    )(xs, xs, xs, mod, g.reshape(1, D), w_in.astype(MXU_DTYPE), conv_w, cos, sin,
      wif_pad, bif_pad, wif.T.astype(MXU_DTYPE), bif.reshape(4 * H, 1))

    def blk(d, j):
        return jnp.where(d == 0, j, jnp.where(j == 0, 0, nt - j))

    cspec = pl.BlockSpec((1, TS, D), lambda b, d, j: (b, blk(d, j), 0))
    hs = pl.pallas_call(
        functools.partial(_ml_scan_kernel, D=D),
        grid=(B, 2, nt),
        in_specs=[cspec, cspec, cspec,
                  pl.BlockSpec((1, TS, 128), lambda b, d, j: (b, blk(d, j), 0)),
                  pl.BlockSpec((1, 4 * H, TS), lambda b, d, j: (b, 0, blk(d, j)))],
        out_specs=pl.BlockSpec((1, 1, TS, D), lambda b, d, j: (d, b, blk(d, j), 0)),
        out_shape=jax.ShapeDtypeStruct((2, B, S, D), F32),
        scratch_shapes=[pltpu.VMEM((H, dh, dh), F32), pltpu.VMEM((H, 1, dh), F32), pltpu.VMEM((8, 128), F32)],
        compiler_params=_cparams(("parallel", "arbitrary", "arbitrary")),
        name="mlstm_scan",
    )(q, k, v, gc, gr)
    dspec = lambda dd: pl.BlockSpec((1, 1, TS, D), lambda b, t: (dd, b, t, 0))
    return pl.pallas_call(
        functools.partial(_ml_post_kernel, D=D),
        grid=(B, nt),
        in_specs=[main, dspec(0), dspec(1), main, mspec, _const_spec((1, D)), _const_spec((D, D))],
        out_specs=main,
        out_shape=jax.ShapeDtypeStruct(xs.shape, F32),
        compiler_params=_cparams(("parallel", "parallel")),
        name="mlstm_post",
    )(xs, hs, hs, o, mod, norm_g.reshape(1, D), w_out.astype(MXU_DTYPE))


def _na_pre_kernel(x_ref, mod_ref, g_ref, w_ref, q_ref, k_ref, v_ref, *, D):
    mod = mod_ref[0, 0]
    h = _norm_mod(x_ref[0], g_ref[...], mod[1:2], mod[0:1]).astype(MXU_DTYPE)
    scale = (D // NA_HEADS) ** -0.5
    q_ref[0] = (_dot(h, w_ref[:, 0:D]) * scale).astype(q_ref.dtype)
    k_ref[0] = _dot(h, w_ref[:, D:2 * D]).astype(k_ref.dtype)
    v_ref[0] = _dot(h, w_ref[:, 2 * D:3 * D]).astype(v_ref.dtype)


def _na_attn_kernel(q_ref, k_ref, v_ref, bias_ref, o_ref, *, nctx, rows):
    t = pl.program_id(1)
    W = GRID_W
    nloc = NA_WIN_ROWS * W
    nq = nctx // W
    lane = lax.broadcasted_iota(jnp.int32, (W, 128), 1)
    r_i = lax.broadcasted_iota(jnp.int32, (2 * W, 128), 0)
    l_i = lax.broadcasted_iota(jnp.int32, (2 * W, 128), 1)
    same_head = jnp.right_shift(r_i, 6) == jnp.right_shift(l_i, 6)

    def pair(p, local):
        sl = slice(p * 128, (p + 1) * 128)
        qp = q_ref[0, :, sl]
        qs = jnp.concatenate([qp, qp], axis=0)
        qs = jnp.where(same_head, qs, jnp.zeros_like(qs))
        s_ctx = _dot_nt(k_ref[0, 0:nctx, sl], qs)
        m = jnp.max(s_ctx, axis=0, keepdims=True)
        if local:
            r = t - nq
            r0 = jnp.clip(r - NA_WIN_ROWS // 2, 0, rows - NA_WIN_ROWS)
            base = pl.multiple_of(nctx + r0 * W, W)
            s_loc = _dot_nt(k_ref[0, pl.ds(base, nloc), sl], qs) + bias_ref[0, p]
            m = jnp.maximum(m, jnp.max(s_loc, axis=0, keepdims=True))
        p_ctx = jnp.exp(s_ctx - m)
        den = jnp.sum(p_ctx, axis=0, keepdims=True)
        if local:
            p_loc = jnp.exp(s_loc - m)
            den = den + jnp.sum(p_loc, axis=0, keepdims=True)
        inv = 1.0 / den
        res = _dot_tn(p_ctx * inv, v_ref[0, 0:nctx, sl])
        if local:
            res = res + _dot_tn(p_loc * inv, v_ref[0, pl.ds(base, nloc), sl])
        o_ref[0, :, sl] = jnp.where(lane < W, res[0:W], res[W:2 * W]).astype(o_ref.dtype)

    @pl.when(t < nq)
    def _():
        for p in range(NA_HEADS // 2):
            pair(p, False)

    @pl.when(t >= nq)
    def _():
        for p in range(NA_HEADS // 2):
            pair(p, True)


def _proj_residual_kernel(x_ref, a_ref, mod_ref, w_ref, o_ref):
    gate = mod_ref[0, 0][2:3]
    o_ref[0] = x_ref[0] + gate * _dot(a_ref[0], w_ref[...])


def _na_bias_tables(rpb):
    H = rpb.shape[0]
    W, KR, KC = GRID_W, NA_WIN_ROWS, NA_WIN_COLS
    cols = jnp.arange(W, dtype=jnp.int32)
    c0 = jnp.clip(cols - KC // 2, 0, W - KC)
    col_mask = (cols[:, None] >= c0[None, :]) & (cols[:, None] < c0[None, :] + KC)
    dc_idx = jnp.clip(cols[:, None] - cols[None, :], 1 - KC, KC - 1) + KC - 1
    onehot = (dc_idx[:, :, None] == jnp.arange(2 * KC - 1)[None, None, :]).astype(F32)
    full = jnp.einsum('hrc,kqc->hrkq', rpb.astype(F32), onehot, precision=lax.Precision.HIGHEST)
    full = jnp.where(col_mask[None, None], full, NEG)
    tabs = []
    for off in range(KR):
        t = full[:, KR - 1 - off:2 * KR - 1 - off].reshape(H // 2, 2, KR, W, W)
        tabs.append(jnp.transpose(t, (0, 2, 3, 1, 4)).reshape(H // 2, KR * W, 2 * W))
    return jnp.stack(tabs, axis=0)


def _na_layer(xs, mod, g, w_qkv, rpb, w_out):
    B, S, D = xs.shape
    nt = S // TS
    W = GRID_W
    rows = (S - TS) // W
    assert rows >= NA_WIN_ROWS
    main, _, _, mspec = _tile_specs(D, nt)
    bf = lambda: jax.ShapeDtypeStruct(xs.shape, MXU_DTYPE)
    q, k, v = pl.pallas_call(
        functools.partial(_na_pre_kernel, D=D),
        grid=(B, nt),
        in_specs=[main, mspec, _const_spec((1, D)), _const_spec((D, 3 * D))],
        out_specs=[main, main, main],
        out_shape=[bf(), bf(), bf()],
        compiler_params=_cparams(("parallel", "parallel")),
        name="na_qkv",
    )(xs, mod, g.reshape(1, D), w_qkv.astype(MXU_DTYPE))
    bias = _na_bias_tables(rpb)
    nq = TS // W

    def variant(t):
        r = t - nq
        r0 = jnp.clip(r - NA_WIN_ROWS // 2, 0, rows - NA_WIN_ROWS)
        return jnp.where(t < nq, 0, r - r0)

    qspec = pl.BlockSpec((1, W, D), lambda b, t: (b, t, 0))
    full = pl.BlockSpec((1, S, D), lambda b, t: (b, 0, 0))
    o = pl.pallas_call(
        functools.partial(_na_attn_kernel, nctx=TS, rows=rows),
        grid=(B, S // W),
        in_specs=[qspec, full, full,
                  pl.BlockSpec((1, NA_HEADS // 2, NA_WIN_ROWS * W, 2 * W), lambda b, t: (variant(t), 0, 0, 0))],
        out_specs=qspec,
        out_shape=bf(),
        compiler_params=_cparams(("parallel", "arbitrary")),
        name="na_attention",
    )(q, k, v, bias)
    return pl.pallas_call(
        _proj_residual_kernel,
        grid=(B, nt),
        in_specs=[main, main, mspec, _const_spec((D, D))],
        out_specs=main,
        out_shape=jax.ShapeDtypeStruct(xs.shape, F32),
        compiler_params=_cparams(("parallel", "parallel")),
        name="na_out",
    )(xs, o, mod, w_out.astype(MXU_DTYPE))


def _final_norm_kernel(x_ref, g_ref, o_ref):
    x = x_ref[0]
    o_ref[0] = x * lax.rsqrt(jnp.mean(x * x, axis=-1, keepdims=True) + EPS) * g_ref[...]


def _final_norm(xs, g):
    B, S, D = xs.shape
    nl = S // TS - 1
    return pl.pallas_call(
        _final_norm_kernel,
        grid=(B, nl),
        in_specs=[pl.BlockSpec((1, TS, D), lambda b, t: (b, t + 1, 0)), _const_spec((1, D))],
        out_specs=pl.BlockSpec((1, TS, D), lambda b, t: (b, t, 0)),
        out_shape=jax.ShapeDtypeStruct((B, S - TS, D), F32),
        compiler_params=_cparams(("parallel", "parallel")),
        name="final_norm",
    )(xs, g.reshape(1, D))


def kernel(x, c, ctx, c_ctx, ada_w, ada_b, norm1_g, norm2_g, ffn_w_in, ffn_conv, ffn_w_out, s5_lam_re, s5_lam_im, s5_log_dt, s5_b_re, s5_b_im, s5_c_re, s5_c_im, s5_d, s5_glu_w, s5_glu_b, ml_w_in, ml_conv, ml_w_if, ml_b_if, ml_norm_g, ml_w_out, na_w_qkv, na_rpb, na_w_out, final_g):
    assert ctx.shape[1] == TS and x.shape[1] % TS == 0 and x.shape[0] % 8 == 0
    depth = ada_w.shape[0]
    xs = jnp.concatenate([ctx, x], axis=1)
    mods = _modulation_all(c, c_ctx, ada_w, ada_b)
    for i in range(depth):
        kind, j = i % 3, i // 3
        if kind == 0:
            params = (s5_lam_re[j], s5_lam_im[j], s5_log_dt[j], s5_b_re[j], s5_b_im[j], s5_c_re[j], s5_c_im[j],
                      s5_d[j])
            xs = _s5_layer(xs, mods[i], norm1_g[i], params, s5_glu_w[j], s5_glu_b[j])
        elif kind == 1:
            xs = _ml_layer(xs, mods[i], norm1_g[i], ml_w_in[j], ml_conv[j], ml_w_if[j], ml_b_if[j],
                           ml_norm_g[j], ml_w_out[j])
        else:
            xs = _na_layer(xs, mods[i], norm1_g[i], na_w_qkv[j], na_rpb[j], na_w_out[j])
        xs = _ffn_layer(xs, mods[i], norm2_g[i], ffn_w_in[i], ffn_conv[i], ffn_w_out[i])
    return _final_norm(xs, final_g)
```

```python
import functools
import math

import jax
import jax.numpy as jnp
from jax import lax
from jax.experimental import pallas as pl
from jax.experimental.pallas import tpu as pltpu

F32 = jnp.float32
MXU_DTYPE = jnp.bfloat16
EPS = 1e-6
NEG = -1e30
TS = 256
HALO = 8
GRID_W = 64
ROPE_BASE = 10000.0
LOG2E = math.log2(math.e)
S5_GROUP = 16
S5_STATE = 64
S5_T = 16
ML_HEADS = 4
NA_HEADS = 16
NA_WIN_ROWS = 8
NA_WIN_COLS = 16
VMEM_LIMIT = 56 * 1024 * 1024


def _cparams(sem):
    return pltpu.CompilerParams(dimension_semantics=sem, vmem_limit_bytes=VMEM_LIMIT)


def _dot(a, b):
    return jnp.dot(a.astype(MXU_DTYPE), b.astype(MXU_DTYPE), preferred_element_type=F32)


def _dot_nt(a, b):
    return lax.dot_general(a.astype(MXU_DTYPE), b.astype(MXU_DTYPE), (((1,), (1,)), ((), ())),
                           preferred_element_type=F32)


def _dot_tn(a, b):
    return lax.dot_general(a.astype(MXU_DTYPE), b.astype(MXU_DTYPE), (((0,), (0,)), ((), ())),
                           preferred_element_type=F32)


def _norm_mod(x, g, sc, sh):
    ms = jnp.mean(x * x, axis=-1, keepdims=True)
    return (x * lax.rsqrt(ms + EPS) * g) * (1.0 + sc) + sh


def _sigmoid(x):
    return 1.0 / (1.0 + jnp.exp(-x))


def _log_sigmoid(x):
    return jnp.minimum(x, 0.0) - jnp.log1p(jnp.exp(-jnp.abs(x)))


def _ext_rows_normed(x, xp, xn, g, sc, sh, t, nt):
    n = x.shape[0]
    xe = jnp.concatenate([xp, x, xn], axis=0)
    rows = lax.broadcasted_iota(jnp.int32, (n + 2 * HALO, 1), 0)
    prev_ok = t >= 2
    next_ok = jnp.logical_and(t >= 1, t <= nt - 2)
    keep = jnp.logical_and(jnp.logical_or(rows >= HALO, prev_ok), jnp.logical_or(rows < HALO + n, next_ok))
    return _norm_mod(xe, g, sc, sh) * keep.astype(F32)


def _conv3_rows(u_ref, c, cw):
    n = u_ref.shape[1] - 2 * HALO
    return (cw[0:1] * u_ref[c, HALO - 1:HALO - 1 + n, :] + cw[1:2] * u_ref[c, HALO:HALO + n, :]
            + cw[2:3] * u_ref[c, HALO + 1:HALO + 1 + n, :])


def _tile_specs(D, nt, first=0):
    per = TS // HALO
    main = pl.BlockSpec((1, TS, D), lambda b, t: (b, t + first, 0))
    prev = pl.BlockSpec((1, HALO, D), lambda b, t: (b, jnp.maximum((t + first) * per - 1, 0), 0))
    nxt = pl.BlockSpec((1, HALO, D), lambda b, t: (b, jnp.minimum((t + first + 1) * per, nt * per - 1), 0))
    mod = pl.BlockSpec((1, 1, 6, D), lambda b, t: (b, jnp.minimum(t + first, 1), 0, 0))
    return main, prev, nxt, mod


def _const_spec(shape):
    nd = len(shape)
    return pl.BlockSpec(shape, lambda *_: (0,) * nd)


def _mod_kernel(c_ref, w_ref, b_ref, o_ref):
    cv = c_ref[...]
    o_ref[0] = _dot(cv * _sigmoid(cv), w_ref[0]) + b_ref[0]


def _modulation_all(c, c_ctx, ada_w, ada_b):
    depth, D, D6 = ada_w.shape
    B = c.shape[0]
    rows = ((B + 1 + 7) // 8) * 8
    cc = jnp.zeros((rows, D), F32).at[:B].set(c).at[B].set(c_ctx)
    nb = 4
    out = pl.pallas_call(
        _mod_kernel,
        grid=(depth, nb),
        in_specs=[pl.BlockSpec((rows, D), lambda i, n: (0, 0)),
                  pl.BlockSpec((1, D, D6 // nb), lambda i, n: (i, 0, n)),
                  pl.BlockSpec((1, 1, D6 // nb), lambda i, n: (i, 0, n))],
        out_specs=pl.BlockSpec((1, rows, D6 // nb), lambda i, n: (i, 0, n)),
        out_shape=jax.ShapeDtypeStruct((depth, rows, D6), F32),
        compiler_params=_cparams(("parallel", "parallel")),
        name="adaln_modulation",
    )(cc, ada_w, ada_b.reshape(depth, 1, D6))
    lat = out[:, :B].reshape(depth, B, 1, 6, D)
    cx = jnp.broadcast_to(out[:, B].reshape(depth, 1, 1, 6, D), (depth, B, 1, 6, D))
    return jnp.concatenate([cx, lat], axis=2)


def _ffn_kernel(x_ref, xp_ref, xn_ref, mod_ref, g_ref, win_ref, cw_ref, wout_ref, *rest, F, FC, nt, first, final):
    fg_ref = rest[0] if final else None
    o_ref, ua_ref, ug_ref, act_ref = rest[-4:]
    t = pl.program_id(1) + first
    mod = mod_ref[0, 0]
    sh, sc, gate = mod[3:4], mod[4:5], mod[5:6]
    x = x_ref[0]
    h = _ext_rows_normed(x, xp_ref[0], xn_ref[0], g_ref[...], sc, sh, t, nt).astype(MXU_DTYPE)
    for c in range(F // FC):
        ua_ref[c] = _dot(h, win_ref[:, c * FC:(c + 1) * FC])
        ug_ref[c] = _dot(h, win_ref[:, F + c * FC:F + (c + 1) * FC])
        a = _conv3_rows(ua_ref, c, cw_ref[:, c * FC:(c + 1) * FC])
        gg = _conv3_rows(ug_ref, c, cw_ref[:, F + c * FC:F + (c + 1) * FC])
        act_ref[:, c * FC:(c + 1) * FC] = (a * (gg * _sigmoid(gg))).astype(act_ref.dtype)
    y = x + gate * _dot(act_ref[...], wout_ref[...])
    if final:
        y = y * lax.rsqrt(jnp.mean(y * y, axis=-1, keepdims=True) + EPS) * fg_ref[...]
    o_ref[0] = y


def _ffn_layer(xs, mod, g, w_in, conv_w, w_out, final_g=None):
    B, S, D = xs.shape
    F = w_out.shape[0]
    FC = 256
    nt = S // TS
    final = final_g is not None
    first = 1 if final else 0
    main, prev, nxt, mspec = _tile_specs(D, nt, first)
    kern = functools.partial(_ffn_kernel, F=F, FC=FC, nt=nt, first=first, final=final)
    ext = pltpu.VMEM((F // FC, TS + 2 * HALO, FC), F32)
    extra_specs, extra_args = ([_const_spec((1, D))], [final_g.reshape(1, D)]) if final else ([], [])
    return pl.pallas_call(
        kern,
        grid=(B, nt - first),
        in_specs=[main, prev, nxt, mspec, _const_spec((1, D)), _const_spec((D, 2 * F)),
                  _const_spec((3, 2 * F)), _const_spec((F, D))] + extra_specs,
        out_specs=pl.BlockSpec((1, TS, D), lambda b, t: (b, t, 0)),
        out_shape=jax.ShapeDtypeStruct((B, S - first * TS, D), F32),
        scratch_shapes=[ext, ext, pltpu.VMEM((TS, F), MXU_DTYPE)],
        compiler_params=_cparams(("parallel", "parallel")),
        name="conv_ffn",
    )(xs, xs, xs, mod, g.reshape(1, D), w_in.astype(MXU_DTYPE), conv_w, w_out.astype(MXU_DTYPE), *extra_args)


def _phase_perm(n, inverse):
    T = S5_T
    p = lax.broadcasted_iota(jnp.int32, (n, n), 0)
    t = lax.broadcasted_iota(jnp.int32, (n, n), 1)
    if inverse:
        p, t = t, p
    return (t == (p % (n // T)) * T + p // (n // T)).astype(MXU_DTYPE)


def _s5_in_kernel(x_ref, mod_ref, g_ref, u_ref):
    n = x_ref.shape[1]
    nch = n // S5_T
    mod = mod_ref[0, 0]
    h = _norm_mod(x_ref[0], g_ref[...], mod[1:2], mod[0:1])
    xp = _dot(_phase_perm(n, False), h)
    seg = jnp.right_shift(lax.broadcasted_iota(jnp.int32, (nch, 128), 1), 4)
    for lb in range(x_ref.shape[2] // 128):
        for sh in range(S5_T // 8):
            pieces = [xp[(sh * 8 + s8) * nch:(sh * 8 + s8 + 1) * nch, lb * 128:(lb + 1) * 128] for s8 in range(8)]
            for j in range(8):
                acc = None
                for s8 in range(8):
                    k = (s8 - j) % 8
                    r = pieces[s8] if k == 0 else pltpu.roll(pieces[s8], k * S5_GROUP, axis=1)
                    acc = r if acc is None else jnp.where(seg == s8, r, acc)
                c0 = (lb * 8 + j) * 256 + sh * 128
                u_ref[0, :, c0:c0 + 128] = acc.astype(u_ref.dtype)


def _s5_core_kernel(u_ref, w1_ref, w2f_ref, w2b_ref, a16_ref, d_ref, o_ref, z_ref, xpf_ref, xpb_ref,
                    *, nb, nctx, nchunks, rb):
    R = nchunks * nb
    w1 = w1_ref[0]

    def proj(i, carry):
        rows = pl.ds(pl.multiple_of(i * rb, rb), rb)
        z = _dot(u_ref[rows, :], w1)
        for q in range(4):
            z_ref[q, rows, :] = z[:, q * 128:(q + 1) * 128]
        return carry

    lax.fori_loop(0, R // rb, proj, 0)

    a16 = a16_ref[0]
    ar, ai = a16[:, :128], a16[:, 128:]
    is_f = lax.broadcasted_iota(jnp.int32, (nb, 128), 1) < S5_STATE

    def step(k, carry):
        xr, xi = carry
        cb = jnp.where(k < nctx, nctx - 1 - k, nchunks - 1 - (k - nctx))
        rf = pl.ds(k, nb, stride=nchunks)
        rbk = pl.ds(cb, nb, stride=nchunks)
        xpf_ref[0, rf, :] = xr
        xpf_ref[1, rf, :] = xi
        xpb_ref[0, rbk, :] = xr
        xpb_ref[1, rbk, :] = xi
        er = jnp.where(is_f, z_ref[2, rf, :], z_ref[2, rbk, :])
        ei = jnp.where(is_f, z_ref[3, rf, :], z_ref[3, rbk, :])
        return ar * xr - ai * xi + er, ar * xi + ai * xr + ei

    zero = jnp.zeros((nb, 128), F32)
    lax.fori_loop(0, nchunks, step, (zero, zero))

    w2f, w2b, dsk = w2f_ref[0], w2b_ref[0], d_ref[0]

    def finish(i, carry):
        rows = pl.ds(pl.multiple_of(i * rb, rb), rb)
        cat = lambda ref, a, b: jnp.concatenate([ref[a, rows, :], ref[b, rows, :]], axis=1)
        y = cat(z_ref, 0, 1) + _dot(cat(xpf_ref, 0, 1), w2f) + _dot(cat(xpb_ref, 0, 1), w2b)
        y = y + dsk * u_ref[rows, :].astype(F32)
        o_ref[rows, :] = jax.nn.gelu(y).astype(o_ref.dtype)
        return carry

    lax.fori_loop(0, R // rb, finish, 0)


def _s5_out_kernel(x_ref, y_ref, mod_ref, w_ref, b_ref, o_ref, yp_ref, *, D):
    n = x_ref.shape[1]
    nch = n // S5_T
    seg = jnp.right_shift(lax.broadcasted_iota(jnp.int32, (nch, 128), 1), 4)
    for lb in range(D // 128):
        for sh in range(S5_T // 8):
            srcs = [y_ref[0, :, (lb * 8 + j) * 256 + sh * 128:(lb * 8 + j) * 256 + (sh + 1) * 128].astype(F32)
                    for j in range(8)]
            for s8 in range(8):
                acc = None
                for j in range(8):
                    k = (j - s8) % 8
                    r = srcs[j] if k == 0 else pltpu.roll(srcs[j], k * S5_GROUP, axis=1)
                    acc = r if acc is None else jnp.where(seg == j, r, acc)
                s = sh * 8 + s8
                yp_ref[s * nch:(s + 1) * nch, lb * 128:(lb + 1) * 128] = acc
    y = _dot(_phase_perm(n, True), yp_ref[...]).astype(MXU_DTYPE)
    gate = mod_ref[0, 0][2:3]
    z = _dot(y, w_ref[...]) + b_ref[...]
    o_ref[0] = x_ref[0] + gate * (z[:, :D] * _sigmoid(z[:, D:]))


def _s5_weights(lam_re, lam_im, log_dt, b_re, b_im, c_re, c_im, d_skip):
    T, P, GC = S5_T, S5_STATE, S5_GROUP
    G = lam_re.shape[1]
    lam_re, lam_im = lam_re.astype(F32), lam_im.astype(F32)
    dt = jnp.exp(log_dt.astype(F32))[..., None]
    lr = jnp.minimum(lam_re, -1e-4)
    er = jnp.exp(lr * dt)
    ang = lam_im * dt
    a_re, a_im = er * jnp.cos(ang), er * jnp.sin(ang)
    den = lr * lr + lam_im * lam_im
    nr = a_re - 1.0
    q_re = (nr * lr + a_im * lam_im) / den
    q_im = (a_im * lr - nr * lam_im) / den
    bb_re = q_re[..., None] * b_re - q_im[..., None] * b_im
    bb_im = q_re[..., None] * b_im + q_im[..., None] * b_re
    tau = jnp.arange(T + 1, dtype=F32)[:, None, None, None]
    mag = jnp.exp(lr * dt * tau)
    p_re, p_im = mag * jnp.cos(ang * tau), mag * jnp.sin(ang * tau)
    c_re, c_im = c_re.astype(F32), c_im.astype(F32)
    m_re = c_re[None] * p_re[:, :, :, None, :] - c_im[None] * p_im[:, :, :, None, :]
    m_im = c_re[None] * p_im[:, :, :, None, :] + c_im[None] * p_re[:, :, :, None, :]
    hi = lax.Precision.HIGHEST
    kern = (jnp.einsum('tdghp,dgpi->tdghi', m_re[:T], bb_re, precision=hi)
            - jnp.einsum('tdghp,dgpi->tdghi', m_im[:T], bb_im, precision=hi))
    s_i = jnp.arange(T)[:, None]
    t_i = jnp.arange(T)[None, :]
    kf = kern[jnp.clip(t_i - s_i, 0, T - 1), 0]
    kb = kern[jnp.clip(s_i - t_i, 0, T - 1), 1]
    kf = jnp.where((t_i >= s_i)[:, :, None, None, None], kf, 0.0)
    kb = jnp.where((s_i >= t_i)[:, :, None, None, None], kb, 0.0)
    w_intra = jnp.transpose(kf + kb, (2, 0, 4, 1, 3)).reshape(G, T * GC, T * GC)
    pf_re, pf_im = p_re[T - 1 - jnp.arange(T), 0], p_im[T - 1 - jnp.arange(T), 0]
    pb_re, pb_im = p_re[:T, 1], p_im[:T, 1]

    def local(pr, pi, br, bi):
        re = pr[..., None] * br[None] - pi[..., None] * bi[None]
        im = pr[..., None] * bi[None] + pi[..., None] * br[None]
        tr = lambda v: jnp.transpose(v, (1, 0, 3, 2)).reshape(G, T * GC, P)
        return tr(re), tr(im)

    f_re, f_im = local(pf_re, pf_im, bb_re[0], bb_im[0])
    g_re, g_im = local(pb_re, pb_im, bb_re[1], bb_im[1])
    w_state = jnp.concatenate([f_re, g_re, f_im, g_im], axis=-1)
    w1 = jnp.concatenate([w_intra, w_state], axis=-1)
    tf = 1 + jnp.arange(T)
    tb = T - jnp.arange(T)

    def readout(mr, mi):
        tr = lambda v: jnp.transpose(v, (1, 3, 0, 2)).reshape(G, P, T * GC)
        return tr(mr), tr(-mi)

    fr, fi = readout(m_re[tf, 0], m_im[tf, 0])
    br_, bi_ = readout(m_re[tb, 1], m_im[tb, 1])
    zp = jnp.zeros_like(fr)
    w2f = jnp.concatenate([fr, zp, fi, zp], axis=1)
    w2b = jnp.concatenate([zp, br_, zp, bi_], axis=1)
    a16 = jnp.concatenate([p_re[T, 0], p_re[T, 1], p_im[T, 0], p_im[T, 1]], axis=-1).reshape(G, 1, 4 * P)
    dsk = jnp.tile(d_skip.astype(F32).reshape(G, 1, GC), (1, 1, T))
    cast = lambda w: w.astype(MXU_DTYPE)
    return cast(w1), cast(w2f), cast(w2b), a16, dsk


def _s5_layer(xs, mod, g, params, glu_w, glu_b):
    B, S, D = xs.shape
    G, T, GC = D // S5_GROUP, S5_T, S5_GROUP
    nt = S // TS
    nchunks, nctx = S // T, TS // T
    main, _, _, mspec = _tile_specs(D, nt)
    R, W = nchunks * B, T * GC
    cspec = pl.BlockSpec((1, TS // T, G * W), lambda b, t: (b, t, 0))
    u = pl.pallas_call(
        _s5_in_kernel,
        grid=(B, nt),
        in_specs=[main, mspec, _const_spec((1, D))],
        out_specs=cspec,
        out_shape=jax.ShapeDtypeStruct((B, nchunks, G * W), MXU_DTYPE),
        compiler_params=_cparams(("parallel", "parallel")),
        name="s5_in",
    )(xs, mod, g.reshape(1, D))
    w1, w2f, w2b, a16, dsk = _s5_weights(*params)
    gspec = lambda shape: pl.BlockSpec((1,) + shape, lambda gi: (gi, 0, 0))
    rspec = pl.BlockSpec((R, W), lambda gi: (0, gi))
    slab = lambda n: pltpu.VMEM((n, R, 128), F32)
    kern = functools.partial(_s5_core_kernel, nb=B, nctx=nctx, nchunks=nchunks, rb=nchunks * 8)
    y = pl.pallas_call(
        kern,
        grid=(G,),
        in_specs=[rspec, gspec((W, 2 * W)), gspec((W, W)), gspec((W, W)), gspec((1, W)), gspec((1, W))],
        out_specs=rspec,
        out_shape=jax.ShapeDtypeStruct((R, G * W), MXU_DTYPE),
        scratch_shapes=[slab(4), slab(2), slab(2)],
        compiler_params=_cparams(("parallel",)),
        name="s5_core",
    )(u.reshape(R, G * W), w1, w2f, w2b, a16, dsk)
    return pl.pallas_call(
        functools.partial(_s5_out_kernel, D=D),
        grid=(B, nt),
        in_specs=[main, cspec, mspec, _const_spec((D, 2 * D)), _const_spec((1, 2 * D))],
        out_specs=main,
        out_shape=jax.ShapeDtypeStruct(xs.shape, F32),
        scratch_shapes=[pltpu.VMEM((TS, D), F32)],
        compiler_params=_cparams(("parallel", "parallel")),
        name="s5_out",
    )(xs, y.reshape(B, nchunks, G * W), mod, glu_w.astype(MXU_DTYPE), glu_b.reshape(1, 2 * D))


def _ml_pre_kernel(x_ref, xp_ref, xn_ref, mod_ref, g_ref, win_ref, cw_ref, cos_ref, sin_ref,
                   wif_ref, bif_ref, wift_ref, bift_ref,
                   q_ref, k_ref, v_ref, o_ref, gc_ref, gr_ref, u_ref, *, D, nt):
    t = pl.program_id(1)
    mod = mod_ref[0, 0]
    sh, sc = mod[0:1], mod[1:2]
    he = _ext_rows_normed(x_ref[0], xp_ref[0], xn_ref[0], g_ref[...], sc, sh, t, nt)
    h = he[HALO:HALO + TS].astype(MXU_DTYPE)
    he = he.astype(MXU_DTYPE)
    dh = D // ML_HEADS
    cos, sin = cos_ref[...], sin_ref[...]
    for part, out_ref, scale in ((0, q_ref, dh ** -0.5), (1, k_ref, 1.0)):
        for hd in range(ML_HEADS):
            off = part * D + hd * dh
            c = part * ML_HEADS + hd
            u_ref[c] = _dot(he, win_ref[:, off:off + dh])
            y = _conv3_rows(u_ref, c, cw_ref[:, off:off + dh])
            y = y * _sigmoid(y)
            sw = jnp.concatenate([pltpu.roll(y[:, i * 128:(i + 1) * 128], 64, axis=1) for i in range(dh // 128)],
                                 axis=1)
            y = y * cos + sw * sin
            out_ref[0, :, hd * dh:(hd + 1) * dh] = (y * scale).astype(out_ref.dtype)
    v_ref[0] = _dot(h, win_ref[:, 2 * D:3 * D]).astype(v_ref.dtype)
    o_ref[0] = _sigmoid(_dot(h, win_ref[:, 3 * D:4 * D])).astype(o_ref.dtype)
    gcol = _dot(h, wif_ref[...]) + bif_ref[...]
    lane = lax.broadcasted_iota(jnp.int32, gcol.shape, 1)
    gc_ref[0] = jnp.where((lane % 8) >= ML_HEADS, _log_sigmoid(gcol), gcol)
    grow = _dot_nt(wift_ref[...], h) + bift_ref[...]
    row = lax.broadcasted_iota(jnp.int32, grow.shape, 0)
    gr_ref[0] = jnp.where((row % 8) >= ML_HEADS, _log_sigmoid(grow), grow)


def _ml_scan_kernel(q_ref, k_ref, v_ref, gc_ref, gr_ref, o_ref, ct_ref, m_ref, *, D):
    d = pl.program_id(1)
    j = pl.program_id(2)
    H = ML_HEADS
    dh = D // H
    T = q_ref.shape[1]

    @pl.when(j == 0)
    def _():
        ct_ref[...] = jnp.zeros_like(ct_ref)
        m_ref[...] = jnp.zeros_like(m_ref)

    fwd = d == 0
    gc = gc_ref[0]
    gr = gr_ref[0]
    r_i = lax.broadcasted_iota(jnp.int32, (T, T), 0)
    c_i = lax.broadcasted_iota(jnp.int32, (T, T), 1)
    allowed = jnp.where(fwd, r_i, c_i) >= jnp.where(fwd, c_i, r_i)
    tri = allowed.astype(F32)
    hi = lax.Precision.HIGHEST
    b_cols = jnp.dot(tri, gc, precision=hi, preferred_element_type=F32)
    b_rows = lax.dot_general(gr, tri, (((1,), (1,)), ((), ())), precision=hi,
                             preferred_element_type=F32)
    for h in range(H):
        pick_c = lambda a, col: jnp.where(fwd, a[:, col:col + 1], a[:, 8 + col:9 + col])
        pick_r = lambda a, rw: jnp.where(fwd, a[rw:rw + 1], a[8 + rw:9 + rw])
        i_col, f_col, bc = pick_c(gc, h), pick_c(gc, H + h), pick_c(b_cols, H + h)
        i_row, br = pick_r(gr, h), pick_r(b_rows, H + h)
        m_prev = m_ref[h:h + 1, 0:1]
        log_d = jnp.where(allowed, bc - br + i_row, NEG)
        log_inter = bc + m_prev
        m_t = jnp.maximum(log_inter, jnp.max(log_d, axis=-1, keepdims=True))
        qh = q_ref[0, :, h * dh:(h + 1) * dh]
        kh = k_ref[0, :, h * dh:(h + 1) * dh]
        vh = v_ref[0, :, h * dh:(h + 1) * dh]
        w_intra = _dot_nt(qh, kh) * jnp.exp(log_d - m_t)
        w_inter = jnp.exp(log_inter - m_t)
        ct = ct_ref[h]
        ones = jnp.ones((T, 128), MXU_DTYPE)
        both = _dot(w_intra, jnp.concatenate([vh, ones], axis=1)) + w_inter * _dot(qh, ct)
        inv = 1.0 / jnp.maximum(jnp.abs(both[:, dh:]), jnp.exp(-m_t))
        for c in range(dh // 128):
            o_ref[0, 0, :, h * dh + c * 128:h * dh + (c + 1) * 128] = (
                both[:, c * 128:(c + 1) * 128] * inv).astype(o_ref.dtype)
        b_end = jnp.sum(f_col, axis=0, keepdims=True)
        log_w = b_end - bc + i_col
        m_new = jnp.maximum(b_end + m_prev, jnp.max(log_w, axis=0, keepdims=True))
        w = jnp.exp(log_w - m_new)
        decay = jnp.exp(b_end + m_prev - m_new)
        wv = jnp.concatenate([w * vh.astype(F32), jnp.broadcast_to(w, (T, 128))], axis=1)
        ct_ref[h] = decay * ct + _dot_tn(kh, wv)
        m_ref[h:h + 1, :] = jnp.broadcast_to(m_new, (1, m_ref.shape[1]))


def _ml_post_kernel(x_ref, hf_ref, hb_ref, o_ref, mod_ref, ng_ref, w_ref, out_ref, *, D):
    gate = mod_ref[0, 0][2:3]
    dh = D // ML_HEADS
    hs = hf_ref[0, 0].astype(F32) + hb_ref[0, 0].astype(F32)
    parts = []
    for h in range(ML_HEADS):
        seg = hs[:, h * dh:(h + 1) * dh]
        parts.append(seg * lax.rsqrt(jnp.mean(seg * seg, axis=-1, keepdims=True) + EPS))
    hn = jnp.concatenate(parts, axis=1) * ng_ref[...]
    out_ref[0] = x_ref[0] + gate * _dot(o_ref[0].astype(F32) * hn, w_ref[...])


def _rope_tables(L, nctx, dh):
    half = dh // 2
    nf = half // 2
    t = jnp.arange(L, dtype=jnp.int32)
    inv = ROPE_BASE ** (-jnp.arange(nf, dtype=F32) / nf)

    def axis(pos):
        ang = pos.astype(F32)[:, None] * inv[None, :]
        c, s = jnp.cos(ang), jnp.sin(ang)
        return jnp.concatenate([c, c], axis=1), jnp.concatenate([-s, s], axis=1)

    cr, sr = axis(t // GRID_W)
    cc, sc = axis(t % GRID_W)
    cos = jnp.concatenate([cr, cc], axis=1)
    sin = jnp.concatenate([sr, sc], axis=1)
    cos = jnp.concatenate([jnp.ones((nctx, dh), F32), cos], axis=0)
    sin = jnp.concatenate([jnp.zeros((nctx, dh), F32), sin], axis=0)
    return cos, sin


def _ml_layer(xs, mod, g, w_in, conv_w, w_if, b_if, norm_g, w_out):
    B, S, D = xs.shape
    H = ML_HEADS
    dh = D // H
    nt = S // TS
    main, prev, nxt, mspec = _tile_specs(D, nt)
    cos, sin = _rope_tables(S - TS, TS, dh)
    wif = jnp.concatenate([w_if[0], w_if[1]], axis=1)
    bif = jnp.concatenate([b_if[0], b_if[1]], axis=0)
    wif_pad = jnp.zeros((D, 128), F32).at[:, :4 * H].set(wif).astype(MXU_DTYPE)
    bif_pad = jnp.zeros((1, 128), F32).at[0, :4 * H].set(bif)
    tabspec = pl.BlockSpec((TS, dh), lambda b, t: (t, 0))
    bf = lambda: jax.ShapeDtypeStruct(xs.shape, MXU_DTYPE)
    q, k, v, o, gc, gr = pl.pallas_call(
        functools.partial(_ml_pre_kernel, D=D, nt=nt),
        grid=(B, nt),
        in_specs=[main, prev, nxt, mspec, _const_spec((1, D)), _const_spec((D, 4 * D)), _const_spec((3, 2 * D)),
                  tabspec, tabspec, _const_spec((D, 128)), _const_spec((1, 128)),
                  _const_spec((4 * H, D)), _const_spec((4 * H, 1))],
        out_specs=[main, main, main, main,
                   pl.BlockSpec((1, TS, 128), lambda b, t: (b, t, 0)),
                   pl.BlockSpec((1, 4 * H, TS), lambda b, t: (b, 0, t))],
        out_shape=[bf(), bf(), bf(), bf(),
                   jax.ShapeDtypeStruct((B, S, 128), F32), jax.ShapeDtypeStruct((B, 4 * H, S), F32)],
        scratch_shapes=[pltpu.VMEM((2 * H, TS + 2 * HALO, dh), F32)],
        compiler_params=_cparams(("parallel", "parallel")),
        name="mlstm_pre",
    )(xs, xs, xs, mod, g.reshape(1, D), w_in.astype(MXU_DTYPE), conv_w, cos, sin,
      wif_pad, bif_pad, wif.T.astype(MXU_DTYPE), bif.reshape(4 * H, 1))

    def blk(d, j):
        return jnp.where(d == 0, j, jnp.where(j == 0, 0, nt - j))

    cspec = pl.BlockSpec((1, TS, D), lambda b, d, j: (b, blk(d, j), 0))
    hs = pl.pallas_call(
        functools.partial(_ml_scan_kernel, D=D),
        grid=(B, 2, nt),
        in_specs=[cspec, cspec, cspec,
                  pl.BlockSpec((1, TS, 128), lambda b, d, j: (b, blk(d, j), 0)),
                  pl.BlockSpec((1, 4 * H, TS), lambda b, d, j: (b, 0, blk(d, j)))],
        out_specs=pl.BlockSpec((1, 1, TS, D), lambda b, d, j: (d, b, blk(d, j), 0)),
        out_shape=jax.ShapeDtypeStruct((2, B, S, D), MXU_DTYPE),
        scratch_shapes=[pltpu.VMEM((H, dh, dh + 128), F32), pltpu.VMEM((8, 128), F32)],
        compiler_params=_cparams(("parallel", "arbitrary", "arbitrary")),
        name="mlstm_scan",
    )(q, k, v, gc, gr)
    dspec = lambda dd: pl.BlockSpec((1, 1, TS, D), lambda b, t: (dd, b, t, 0))
    return pl.pallas_call(
        functools.partial(_ml_post_kernel, D=D),
        grid=(B, nt),
        in_specs=[main, dspec(0), dspec(1), main, mspec, _const_spec((1, D)), _const_spec((D, D))],
        out_specs=main,
        out_shape=jax.ShapeDtypeStruct(xs.shape, F32),
        compiler_params=_cparams(("parallel", "parallel")),
        name="mlstm_post",
    )(xs, hs, hs, o, mod, norm_g.reshape(1, D), w_out.astype(MXU_DTYPE))


def _na_pre_kernel(x_ref, mod_ref, g_ref, w_ref, q_ref, k_ref, v_ref, *, D):
    mod = mod_ref[0, 0]
    h = _norm_mod(x_ref[0], g_ref[...], mod[1:2], mod[0:1]).astype(MXU_DTYPE)
    scale = LOG2E * (D // NA_HEADS) ** -0.5
    q_ref[0] = (_dot(h, w_ref[:, 0:D]) * scale).astype(q_ref.dtype)
    k_ref[0] = _dot(h, w_ref[:, D:2 * D]).astype(k_ref.dtype)
    v_ref[0] = _dot(h, w_ref[:, 2 * D:3 * D]).astype(v_ref.dtype)


def _na_attn_kernel(q_ref, k_ref, v_ref, bias_ref, o_ref, s_ref, m_ref, *, nctx, rows):
    t = pl.program_id(1)
    W = GRID_W
    nloc = NA_WIN_ROWS * W
    nq = nctx // W
    npairs = NA_HEADS // 2
    lane = lax.broadcasted_iota(jnp.int32, (W, 128), 1)
    r_i = lax.broadcasted_iota(jnp.int32, (2 * W, 128), 0)
    l_i = lax.broadcasted_iota(jnp.int32, (2 * W, 128), 1)
    same_head = jnp.right_shift(r_i, 6) == jnp.right_shift(l_i, 6)
    r0 = jnp.clip(t - nq - NA_WIN_ROWS // 2, 0, rows - NA_WIN_ROWS)
    base = pl.multiple_of(nctx + r0 * W, W)

    def scores(p, local):
        sl = slice(p * 128, (p + 1) * 128)
        qp = q_ref[0, :, sl]
        qs = jnp.concatenate([qp, qp], axis=0)
        qs = jnp.where(same_head, qs, jnp.zeros_like(qs))
        s_ctx = _dot_nt(k_ref[0, 0:nctx, sl], qs)
        s_ref[p, 0:nctx, :] = s_ctx
        m = jnp.max(s_ctx, axis=0, keepdims=True)
        if local:
            s_loc = _dot_nt(k_ref[0, pl.ds(base, nloc), sl], qs) + bias_ref[0, p]
            s_ref[p, nctx:nctx + nloc, :] = s_loc
            m = jnp.maximum(m, jnp.max(s_loc, axis=0, keepdims=True))
        m_ref[p, 0:1, :] = m

    def attend(p, local):
        sl = slice(p * 128, (p + 1) * 128)
        m = m_ref[p, 0:1, :]
        p_ctx = jnp.exp2(s_ref[p, 0:nctx, :] - m).astype(MXU_DTYPE)
        v_ctx = jnp.concatenate([v_ref[0, 0:nctx, sl], jnp.ones((nctx, 128), MXU_DTYPE)], axis=1)
        res = _dot_tn(p_ctx, v_ctx)
        if local:
            p_loc = jnp.exp2(s_ref[p, nctx:nctx + nloc, :] - m).astype(MXU_DTYPE)
            v_loc = jnp.concatenate([v_ref[0, pl.ds(base, nloc), sl], jnp.ones((nloc, 128), MXU_DTYPE)], axis=1)
            res = res + _dot_tn(p_loc, v_loc)
        out = res[:, 0:128] / res[:, 128:256]
        o_ref[0, :, sl] = jnp.where(lane < W, out[0:W], out[W:2 * W]).astype(o_ref.dtype)

    for local in (False, True):
        @pl.when((t >= nq) if local else (t < nq))
        def _():
            for p in range(npairs):
                scores(p, local)
            for p in range(npairs):
                attend(p, local)


def _proj_residual_kernel(x_ref, a_ref, mod_ref, w_ref, o_ref):
    gate = mod_ref[0, 0][2:3]
    o_ref[0] = x_ref[0] + gate * _dot(a_ref[0], w_ref[...])


def _na_bias_tables(rpb):
    H = rpb.shape[0]
    W, KR, KC = GRID_W, NA_WIN_ROWS, NA_WIN_COLS
    cols = jnp.arange(W, dtype=jnp.int32)
    c0 = jnp.clip(cols - KC // 2, 0, W - KC)
    col_mask = (cols[:, None] >= c0[None, :]) & (cols[:, None] < c0[None, :] + KC)
    dc_idx = jnp.clip(cols[:, None] - cols[None, :], 1 - KC, KC - 1) + KC - 1
    onehot = (dc_idx[:, :, None] == jnp.arange(2 * KC - 1)[None, None, :]).astype(F32)
    full = jnp.einsum('hrc,kqc->hrkq', rpb.astype(F32), onehot, precision=lax.Precision.HIGHEST)
    full = jnp.where(col_mask[None, None], full, NEG)
    tabs = []
    for off in range(KR):
        t = full[:, KR - 1 - off:2 * KR - 1 - off].reshape(H // 2, 2, KR, W, W)
        tabs.append(jnp.transpose(t, (0, 2, 3, 1, 4)).reshape(H // 2, KR * W, 2 * W))
    return jnp.stack(tabs, axis=0) * LOG2E


def _na_layer(xs, mod, g, w_qkv, rpb, w_out):
    B, S, D = xs.shape
    nt = S // TS
    W = GRID_W
    rows = (S - TS) // W
    assert rows >= NA_WIN_ROWS
    main, _, _, mspec = _tile_specs(D, nt)
    bf = lambda: jax.ShapeDtypeStruct(xs.shape, MXU_DTYPE)
    q, k, v = pl.pallas_call(
        functools.partial(_na_pre_kernel, D=D),
        grid=(B, nt),
        in_specs=[main, mspec, _const_spec((1, D)), _const_spec((D, 3 * D))],
        out_specs=[main, main, main],
        out_shape=[bf(), bf(), bf()],
        compiler_params=_cparams(("parallel", "parallel")),
        name="na_qkv",
    )(xs, mod, g.reshape(1, D), w_qkv.astype(MXU_DTYPE))
    bias = _na_bias_tables(rpb)
    nq = TS // W

    def variant(t):
        r = t - nq
        r0 = jnp.clip(r - NA_WIN_ROWS // 2, 0, rows - NA_WIN_ROWS)
        return jnp.where(t < nq, 0, r - r0)

    qspec = pl.BlockSpec((1, W, D), lambda b, t: (b, t, 0))
    full = pl.BlockSpec((1, S, D), lambda b, t: (b, 0, 0))
    o = pl.pallas_call(
        functools.partial(_na_attn_kernel, nctx=TS, rows=rows),
        grid=(B, S // W),
        in_specs=[qspec, full, full,
                  pl.BlockSpec((1, NA_HEADS // 2, NA_WIN_ROWS * W, 2 * W), lambda b, t: (variant(t), 0, 0, 0))],
        out_specs=qspec,
        out_shape=bf(),
        scratch_shapes=[pltpu.VMEM((NA_HEADS // 2, TS + NA_WIN_ROWS * W, 128), F32),
                        pltpu.VMEM((NA_HEADS // 2, 8, 128), F32)],
        compiler_params=_cparams(("parallel", "arbitrary")),
        name="na_attention",
    )(q, k, v, bias)
    return pl.pallas_call(
        _proj_residual_kernel,
        grid=(B, nt),
        in_specs=[main, main, mspec, _const_spec((D, D))],
        out_specs=main,
        out_shape=jax.ShapeDtypeStruct(xs.shape, F32),
        compiler_params=_cparams(("parallel", "parallel")),
        name="na_out",
    )(xs, o, mod, w_out.astype(MXU_DTYPE))


def kernel(x, c, ctx, c_ctx, ada_w, ada_b, norm1_g, norm2_g, ffn_w_in, ffn_conv, ffn_w_out, s5_lam_re, s5_lam_im, s5_log_dt, s5_b_re, s5_b_im, s5_c_re, s5_c_im, s5_d, s5_glu_w, s5_glu_b, ml_w_in, ml_conv, ml_w_if, ml_b_if, ml_norm_g, ml_w_out, na_w_qkv, na_rpb, na_w_out, final_g):
    assert ctx.shape[1] == TS and x.shape[1] % TS == 0 and x.shape[0] % 8 == 0
    depth = ada_w.shape[0]
    xs = jnp.concatenate([ctx, x], axis=1)
    mods = _modulation_all(c, c_ctx, ada_w, ada_b)
    for i in range(depth):
        kind, j = i % 3, i // 3
        if kind == 0:
            params = (s5_lam_re[j], s5_lam_im[j], s5_log_dt[j], s5_b_re[j], s5_b_im[j], s5_c_re[j], s5_c_im[j],
                      s5_d[j])
            xs = _s5_layer(xs, mods[i], norm1_g[i], params, s5_glu_w[j], s5_glu_b[j])
        elif kind == 1:
            xs = _ml_layer(xs, mods[i], norm1_g[i], ml_w_in[j], ml_conv[j], ml_w_if[j], ml_b_if[j],
                           ml_norm_g[j], ml_w_out[j])
        else:
            xs = _na_layer(xs, mods[i], norm1_g[i], na_w_qkv[j], na_rpb[j], na_w_out[j])
        xs = _ffn_layer(xs, mods[i], norm2_g[i], ffn_w_in[i], ffn_conv[i], ffn_w_out[i],
                        final_g if i == depth - 1 else None)
    return xs
```

```python
import functools
import math

import jax
import jax.numpy as jnp
from jax import lax
from jax.experimental import pallas as pl
from jax.experimental.pallas import tpu as pltpu

F32 = jnp.float32
MXU_DTYPE = jnp.bfloat16
EPS = 1e-6
NEG = -1e30
TS = 256
HALO = 8
GRID_W = 64
ROPE_BASE = 10000.0
LOG2E = math.log2(math.e)
S5_GROUP = 16
S5_STATE = 64
S5_T = 16
ML_HEADS = 4
NA_HEADS = 16
NA_WIN_ROWS = 8
NA_WIN_COLS = 16
VMEM_LIMIT = 56 * 1024 * 1024


def _cparams(sem):
    return pltpu.CompilerParams(dimension_semantics=sem, vmem_limit_bytes=VMEM_LIMIT)


def _dot(a, b):
    return jnp.dot(a.astype(MXU_DTYPE), b.astype(MXU_DTYPE), preferred_element_type=F32)


def _dot_nt(a, b):
    return lax.dot_general(a.astype(MXU_DTYPE), b.astype(MXU_DTYPE), (((1,), (1,)), ((), ())),
                           preferred_element_type=F32)


def _dot_tn(a, b):
    return lax.dot_general(a.astype(MXU_DTYPE), b.astype(MXU_DTYPE), (((0,), (0,)), ((), ())),
                           preferred_element_type=F32)


def _norm_mod(x, g, sc, sh):
    ms = jnp.mean(x * x, axis=-1, keepdims=True)
    return (x * lax.rsqrt(ms + EPS) * g) * (1.0 + sc) + sh


def _sigmoid(x):
    return 0.5 * jnp.tanh(0.5 * x) + 0.5


def _log_sigmoid(x):
    return jnp.minimum(x, 0.0) - jnp.log1p(jnp.exp(-jnp.abs(x)))


def _ext_rows_normed(x, xp, xn, g, sc, sh, t, nt):
    n = x.shape[0]
    xe = jnp.concatenate([xp, x, xn], axis=0)
    rows = lax.broadcasted_iota(jnp.int32, (n + 2 * HALO, 1), 0)
    prev_ok = t >= 2
    next_ok = jnp.logical_and(t >= 1, t <= nt - 2)
    keep = jnp.logical_and(jnp.logical_or(rows >= HALO, prev_ok), jnp.logical_or(rows < HALO + n, next_ok))
    return _norm_mod(xe, g, sc, sh) * keep.astype(F32)


def _conv3_rows(u_ref, c, cw):
    n = u_ref.shape[1] - 2 * HALO
    return (cw[0:1] * u_ref[c, HALO - 1:HALO - 1 + n, :] + cw[1:2] * u_ref[c, HALO:HALO + n, :]
            + cw[2:3] * u_ref[c, HALO + 1:HALO + 1 + n, :])


def _tile_specs(D, nt, first=0):
    per = TS // HALO
    main = pl.BlockSpec((1, TS, D), lambda b, t: (b, t + first, 0))
    prev = pl.BlockSpec((1, HALO, D), lambda b, t: (b, jnp.maximum((t + first) * per - 1, 0), 0))
    nxt = pl.BlockSpec((1, HALO, D), lambda b, t: (b, jnp.minimum((t + first + 1) * per, nt * per - 1), 0))
    mod = pl.BlockSpec((1, 1, 6, D), lambda b, t: (b, jnp.minimum(t + first, 1), 0, 0))
    return main, prev, nxt, mod


def _const_spec(shape):
    nd = len(shape)
    return pl.BlockSpec(shape, lambda *_: (0,) * nd)


def _mod_kernel(c_ref, w_ref, b_ref, o_ref):
    cv = c_ref[...]
    o_ref[0] = _dot(cv * _sigmoid(cv), w_ref[0]) + b_ref[0]


def _modulation_all(c, c_ctx, ada_w, ada_b):
    depth, D, D6 = ada_w.shape
    B = c.shape[0]
    rows = ((B + 1 + 7) // 8) * 8
    cc = jnp.zeros((rows, D), F32).at[:B].set(c).at[B].set(c_ctx)
    nb = 4
    out = pl.pallas_call(
        _mod_kernel,
        grid=(depth, nb),
        in_specs=[pl.BlockSpec((rows, D), lambda i, n: (0, 0)),
                  pl.BlockSpec((1, D, D6 // nb), lambda i, n: (i, 0, n)),
                  pl.BlockSpec((1, 1, D6 // nb), lambda i, n: (i, 0, n))],
        out_specs=pl.BlockSpec((1, rows, D6 // nb), lambda i, n: (i, 0, n)),
        out_shape=jax.ShapeDtypeStruct((depth, rows, D6), F32),
        compiler_params=_cparams(("parallel", "parallel")),
        name="adaln_modulation",
    )(cc, ada_w, ada_b.reshape(depth, 1, D6))
    lat = out[:, :B].reshape(depth, B, 1, 6, D)
    cx = jnp.broadcast_to(out[:, B].reshape(depth, 1, 1, 6, D), (depth, B, 1, 6, D))
    return jnp.concatenate([cx, lat], axis=2)


def _ffn_kernel(x_ref, xp_ref, xn_ref, mod_ref, g_ref, win_ref, cw_ref, wout_ref, *rest, F, FC, nt, first, final):
    fg_ref = rest[0] if final else None
    o_ref, ua_ref, ug_ref, act_ref = rest[-4:]
    t = pl.program_id(1) + first
    mod = mod_ref[0, 0]
    sh, sc, gate = mod[3:4], mod[4:5], mod[5:6]
    x = x_ref[0]
    h = _ext_rows_normed(x, xp_ref[0], xn_ref[0], g_ref[...], sc, sh, t, nt).astype(MXU_DTYPE)
    for c in range(F // FC):
        ua_ref[c] = _dot(h, win_ref[:, c * FC:(c + 1) * FC])
        ug_ref[c] = _dot(h, win_ref[:, F + c * FC:F + (c + 1) * FC])
        a = _conv3_rows(ua_ref, c, cw_ref[:, c * FC:(c + 1) * FC])
        gg = _conv3_rows(ug_ref, c, cw_ref[:, F + c * FC:F + (c + 1) * FC])
        act_ref[:, c * FC:(c + 1) * FC] = (a * (gg * _sigmoid(gg))).astype(act_ref.dtype)
    y = x + gate * _dot(act_ref[...], wout_ref[...])
    if final:
        y = y * lax.rsqrt(jnp.mean(y * y, axis=-1, keepdims=True) + EPS) * fg_ref[...]
    o_ref[0] = y


def _ffn_layer(xs, mod, g, w_in, conv_w, w_out, final_g=None):
    B, S, D = xs.shape
    F = w_out.shape[0]
    FC = 256
    nt = S // TS
    final = final_g is not None
    first = 1 if final else 0
    main, prev, nxt, mspec = _tile_specs(D, nt, first)
    kern = functools.partial(_ffn_kernel, F=F, FC=FC, nt=nt, first=first, final=final)
    ext = pltpu.VMEM((F // FC, TS + 2 * HALO, FC), F32)
    extra_specs, extra_args = ([_const_spec((1, D))], [final_g.reshape(1, D)]) if final else ([], [])
    return pl.pallas_call(
        kern,
        grid=(B, nt - first),
        in_specs=[main, prev, nxt, mspec, _const_spec((1, D)), _const_spec((D, 2 * F)),
                  _const_spec((3, 2 * F)), _const_spec((F, D))] + extra_specs,
        out_specs=pl.BlockSpec((1, TS, D), lambda b, t: (b, t, 0)),
        out_shape=jax.ShapeDtypeStruct((B, S - first * TS, D), F32),
        scratch_shapes=[ext, ext, pltpu.VMEM((TS, F), MXU_DTYPE)],
        compiler_params=_cparams(("parallel", "parallel")),
        name="conv_ffn",
    )(xs, xs, xs, mod, g.reshape(1, D), w_in.astype(MXU_DTYPE), conv_w, w_out.astype(MXU_DTYPE), *extra_args)


def _phase_perm(n, inverse):
    T = S5_T
    p = lax.broadcasted_iota(jnp.int32, (n, n), 0)
    t = lax.broadcasted_iota(jnp.int32, (n, n), 1)
    if inverse:
        p, t = t, p
    return (t == (p % (n // T)) * T + p // (n // T)).astype(MXU_DTYPE)


def _s5_in_kernel(x_ref, mod_ref, g_ref, u_ref):
    n = x_ref.shape[1]
    nch = n // S5_T
    mod = mod_ref[0, 0]
    h = _norm_mod(x_ref[0], g_ref[...], mod[1:2], mod[0:1])
    xp = _dot(_phase_perm(n, False), h)
    seg = jnp.right_shift(lax.broadcasted_iota(jnp.int32, (nch, 128), 1), 4)
    for lb in range(x_ref.shape[2] // 128):
        for sh in range(S5_T // 8):
            pieces = [xp[(sh * 8 + s8) * nch:(sh * 8 + s8 + 1) * nch, lb * 128:(lb + 1) * 128] for s8 in range(8)]
            for j in range(8):
                acc = None
                for s8 in range(8):
                    k = (s8 - j) % 8
                    r = pieces[s8] if k == 0 else pltpu.roll(pieces[s8], k * S5_GROUP, axis=1)
                    acc = r if acc is None else jnp.where(seg == s8, r, acc)
                c0 = (lb * 8 + j) * 256 + sh * 128
                u_ref[0, :, c0:c0 + 128] = acc.astype(u_ref.dtype)


def _s5_core_kernel(u_ref, w1_ref, w2f_ref, w2b_ref, a16_ref, d_ref, o_ref, z_ref, e_ref, xpf_ref, xpb_ref,
                    *, nb, nctx, nchunks, pitch):
    bb = 8
    rb = bb * nchunks
    w1 = w1_ref[0]

    def batch_rows(q, b):
        return (q, pl.ds(b, nchunks, stride=pitch), slice(None))

    def proj(i, carry):
        rows = pl.ds(pl.multiple_of(i * rb, rb), rb)
        z = _dot(u_ref[rows, :], w1)
        for q in range(2):
            z_ref[q, rows, :] = z[:, q * 128:(q + 1) * 128]
            for bl in range(bb):
                e_ref[batch_rows(q, i * bb + bl)] = z[bl * nchunks:(bl + 1) * nchunks, (q + 2) * 128:(q + 3) * 128]
        return carry

    lax.fori_loop(0, nb // bb, proj, 0)

    a16 = a16_ref[0]
    ar, ai = a16[:, :128], a16[:, 128:]
    is_f = lax.broadcasted_iota(jnp.int32, (nb, 128), 1) < S5_STATE

    def step(k, carry):
        xr, xi = carry
        cb = jnp.where(k < nctx, nctx - 1 - k, nchunks - 1 - (k - nctx))
        rf = pl.ds(pl.multiple_of(k * pitch, 8), nb)
        rbk = pl.ds(pl.multiple_of(cb * pitch, 8), nb)
        xpf_ref[0, rf, :] = xr
        xpf_ref[1, rf, :] = xi
        xpb_ref[0, rbk, :] = xr
        xpb_ref[1, rbk, :] = xi
        er = jnp.where(is_f, e_ref[0, rf, :], e_ref[0, rbk, :])
        ei = jnp.where(is_f, e_ref[1, rf, :], e_ref[1, rbk, :])
        return ar * xr - ai * xi + er, ar * xi + ai * xr + ei

    zero = jnp.zeros((nb, 128), F32)
    lax.fori_loop(0, nchunks, step, (zero, zero))

    w2f, w2b, dsk = w2f_ref[0], w2b_ref[0], d_ref[0]

    def finish(i, carry):
        rows = pl.ds(pl.multiple_of(i * rb, rb), rb)

        def entering(ref):
            return jnp.concatenate(
                [jnp.concatenate([ref[batch_rows(q, i * bb + bl)] for bl in range(bb)], axis=0)
                 for q in range(2)], axis=1)

        y = jnp.concatenate([z_ref[0, rows, :], z_ref[1, rows, :]], axis=1)
        y = y + _dot(entering(xpf_ref), w2f) + _dot(entering(xpb_ref), w2b)
        y = y + dsk * u_ref[rows, :].astype(F32)
        o_ref[rows, :] = jax.nn.gelu(y).astype(o_ref.dtype)
        return carry

    lax.fori_loop(0, nb // bb, finish, 0)


def _s5_out_kernel(x_ref, y_ref, mod_ref, w_ref, b_ref, o_ref, yp_ref, *, D):
    n = x_ref.shape[1]
    nch = n // S5_T
    seg = jnp.right_shift(lax.broadcasted_iota(jnp.int32, (nch, 128), 1), 4)
    for lb in range(D // 128):
        for sh in range(S5_T // 8):
            srcs = [y_ref[0, :, (lb * 8 + j) * 256 + sh * 128:(lb * 8 + j) * 256 + (sh + 1) * 128].astype(F32)
                    for j in range(8)]
            for s8 in range(8):
                acc = None
                for j in range(8):
                    k = (j - s8) % 8
                    r = srcs[j] if k == 0 else pltpu.roll(srcs[j], k * S5_GROUP, axis=1)
                    acc = r if acc is None else jnp.where(seg == j, r, acc)
                s = sh * 8 + s8
                yp_ref[s * nch:(s + 1) * nch, lb * 128:(lb + 1) * 128] = acc
    y = _dot(_phase_perm(n, True), yp_ref[...]).astype(MXU_DTYPE)
    gate = mod_ref[0, 0][2:3]
    z = _dot(y, w_ref[...]) + b_ref[...]
    o_ref[0] = x_ref[0] + gate * (z[:, :D] * _sigmoid(z[:, D:]))


def _s5_weights(lam_re, lam_im, log_dt, b_re, b_im, c_re, c_im, d_skip):
    T, P, GC = S5_T, S5_STATE, S5_GROUP
    G = lam_re.shape[1]
    lam_re, lam_im = lam_re.astype(F32), lam_im.astype(F32)
    dt = jnp.exp(log_dt.astype(F32))[..., None]
    lr = jnp.minimum(lam_re, -1e-4)
    er = jnp.exp(lr * dt)
    ang = lam_im * dt
    a_re, a_im = er * jnp.cos(ang), er * jnp.sin(ang)
    den = lr * lr + lam_im * lam_im
    nr = a_re - 1.0
    q_re = (nr * lr + a_im * lam_im) / den
    q_im = (a_im * lr - nr * lam_im) / den
    bb_re = q_re[..., None] * b_re - q_im[..., None] * b_im
    bb_im = q_re[..., None] * b_im + q_im[..., None] * b_re
    tau = jnp.arange(T + 1, dtype=F32)[:, None, None, None]
    mag = jnp.exp(lr * dt * tau)
    p_re, p_im = mag * jnp.cos(ang * tau), mag * jnp.sin(ang * tau)
    c_re, c_im = c_re.astype(F32), c_im.astype(F32)
    m_re = c_re[None] * p_re[:, :, :, None, :] - c_im[None] * p_im[:, :, :, None, :]
    m_im = c_re[None] * p_im[:, :, :, None, :] + c_im[None] * p_re[:, :, :, None, :]
    hi = lax.Precision.HIGHEST
    kern = (jnp.einsum('tdghp,dgpi->tdghi', m_re[:T], bb_re, precision=hi)
            - jnp.einsum('tdghp,dgpi->tdghi', m_im[:T], bb_im, precision=hi))
    s_i = jnp.arange(T)[:, None]
    t_i = jnp.arange(T)[None, :]
    kf = kern[jnp.clip(t_i - s_i, 0, T - 1), 0]
    kb = kern[jnp.clip(s_i - t_i, 0, T - 1), 1]
    kf = jnp.where((t_i >= s_i)[:, :, None, None, None], kf, 0.0)
    kb = jnp.where((s_i >= t_i)[:, :, None, None, None], kb, 0.0)
    w_intra = jnp.transpose(kf + kb, (2, 0, 4, 1, 3)).reshape(G, T * GC, T * GC)
    pf_re, pf_im = p_re[T - 1 - jnp.arange(T), 0], p_im[T - 1 - jnp.arange(T), 0]
    pb_re, pb_im = p_re[:T, 1], p_im[:T, 1]

    def local(pr, pi, br, bi):
        re = pr[..., None] * br[None] - pi[..., None] * bi[None]
        im = pr[..., None] * bi[None] + pi[..., None] * br[None]
        tr = lambda v: jnp.transpose(v, (1, 0, 3, 2)).reshape(G, T * GC, P)
        return tr(re), tr(im)

    f_re, f_im = local(pf_re, pf_im, bb_re[0], bb_im[0])
    g_re, g_im = local(pb_re, pb_im, bb_re[1], bb_im[1])
    w_state = jnp.concatenate([f_re, g_re, f_im, g_im], axis=-1)
    w1 = jnp.concatenate([w_intra, w_state], axis=-1)
    tf = 1 + jnp.arange(T)
    tb = T - jnp.arange(T)

    def readout(mr, mi):
        tr = lambda v: jnp.transpose(v, (1, 3, 0, 2)).reshape(G, P, T * GC)
        return tr(mr), tr(-mi)

    fr, fi = readout(m_re[tf, 0], m_im[tf, 0])
    br_, bi_ = readout(m_re[tb, 1], m_im[tb, 1])
    zp = jnp.zeros_like(fr)
    w2f = jnp.concatenate([fr, zp, fi, zp], axis=1)
    w2b = jnp.concatenate([zp, br_, zp, bi_], axis=1)
    a16 = jnp.concatenate([p_re[T, 0], p_re[T, 1], p_im[T, 0], p_im[T, 1]], axis=-1).reshape(G, 1, 4 * P)
    dsk = jnp.tile(d_skip.astype(F32).reshape(G, 1, GC), (1, 1, T))
    cast = lambda w: w.astype(MXU_DTYPE)
    return cast(w1), cast(w2f), cast(w2b), a16, dsk


def _s5_layer(xs, mod, g, params, glu_w, glu_b):
    B, S, D = xs.shape
    G, T, GC = D // S5_GROUP, S5_T, S5_GROUP
    nt = S // TS
    nchunks, nctx = S // T, TS // T
    main, _, _, mspec = _tile_specs(D, nt)
    R, W = nchunks * B, T * GC
    cspec = pl.BlockSpec((1, TS // T, G * W), lambda b, t: (b, t, 0))
    u = pl.pallas_call(
        _s5_in_kernel,
        grid=(B, nt),
        in_specs=[main, mspec, _const_spec((1, D))],
        out_specs=cspec,
        out_shape=jax.ShapeDtypeStruct((B, nchunks, G * W), MXU_DTYPE),
        compiler_params=_cparams(("parallel", "parallel")),
        name="s5_in",
    )(xs, mod, g.reshape(1, D))
    w1, w2f, w2b, a16, dsk = _s5_weights(*params)
    gspec = lambda shape: pl.BlockSpec((1,) + shape, lambda gi: (gi, 0, 0))
    rspec = pl.BlockSpec((R, W), lambda gi: (0, gi))
    pitch = B + 8
    slab = lambda rows: pltpu.VMEM((2, rows, 128), F32)
    kern = functools.partial(_s5_core_kernel, nb=B, nctx=nctx, nchunks=nchunks, pitch=pitch)
    y = pl.pallas_call(
        kern,
        grid=(G,),
        in_specs=[rspec, gspec((W, 2 * W)), gspec((W, W)), gspec((W, W)), gspec((1, W)), gspec((1, W))],
        out_specs=rspec,
        out_shape=jax.ShapeDtypeStruct((R, G * W), MXU_DTYPE),
        scratch_shapes=[slab(R), slab(nchunks * pitch), slab(nchunks * pitch), slab(nchunks * pitch)],
        compiler_params=_cparams(("parallel",)),
        name="s5_core",
    )(u.reshape(R, G * W), w1, w2f, w2b, a16, dsk)
    return pl.pallas_call(
        functools.partial(_s5_out_kernel, D=D),
        grid=(B, nt),
        in_specs=[main, cspec, mspec, _const_spec((D, 2 * D)), _const_spec((1, 2 * D))],
        out_specs=main,
        out_shape=jax.ShapeDtypeStruct(xs.shape, F32),
        scratch_shapes=[pltpu.VMEM((TS, D), F32)],
        compiler_params=_cparams(("parallel", "parallel")),
        name="s5_out",
    )(xs, y.reshape(B, nchunks, G * W), mod, glu_w.astype(MXU_DTYPE), glu_b.reshape(1, 2 * D))


def _ml_pre_kernel(x_ref, xp_ref, xn_ref, mod_ref, g_ref, win_ref, cw_ref, cos_ref, sin_ref,
                   wif_ref, bif_ref, wift_ref, bift_ref,
                   q_ref, k_ref, v_ref, o_ref, gc_ref, gr_ref, u_ref, *, D, nt):
    t = pl.program_id(1)
    mod = mod_ref[0, 0]
    sh, sc = mod[0:1], mod[1:2]
    he = _ext_rows_normed(x_ref[0], xp_ref[0], xn_ref[0], g_ref[...], sc, sh, t, nt)
    h = he[HALO:HALO + TS].astype(MXU_DTYPE)
    he = he.astype(MXU_DTYPE)
    dh = D // ML_HEADS
    cos, sin = cos_ref[...], sin_ref[...]
    for part, out_ref, scale in ((0, q_ref, dh ** -0.5), (1, k_ref, 1.0)):
        for hd in range(ML_HEADS):
            off = part * D + hd * dh
            c = part * ML_HEADS + hd
            u_ref[c] = _dot(he, win_ref[:, off:off + dh])
            y = _conv3_rows(u_ref, c, cw_ref[:, off:off + dh])
            y = y * _sigmoid(y)
            sw = jnp.concatenate([pltpu.roll(y[:, i * 128:(i + 1) * 128], 64, axis=1) for i in range(dh // 128)],
                                 axis=1)
            y = y * cos + sw * sin
            out_ref[0, :, hd * dh:(hd + 1) * dh] = (y * scale).astype(out_ref.dtype)
    v_ref[0] = _dot(h, win_ref[:, 2 * D:3 * D]).astype(v_ref.dtype)
    o_ref[0] = _sigmoid(_dot(h, win_ref[:, 3 * D:4 * D])).astype(o_ref.dtype)
    gcol = _dot(h, wif_ref[...]) + bif_ref[...]
    lane = lax.broadcasted_iota(jnp.int32, gcol.shape, 1)
    gc_ref[0] = jnp.where((lane % 8) >= ML_HEADS, _log_sigmoid(gcol), gcol)
    grow = _dot_nt(wift_ref[...], h) + bift_ref[...]
    row = lax.broadcasted_iota(jnp.int32, grow.shape, 0)
    gr_ref[0] = jnp.where((row % 8) >= ML_HEADS, _log_sigmoid(grow), grow)


def _ml_scan_kernel(q_ref, k_ref, v_ref, gc_ref, gr_ref, o_ref, ct_ref, m_ref, *, D):
    d = pl.program_id(1)
    j = pl.program_id(2)
    H = ML_HEADS
    dh = D // H
    T = q_ref.shape[1]

    @pl.when(j == 0)
    def _():
        ct_ref[...] = jnp.zeros_like(ct_ref)
        m_ref[...] = jnp.zeros_like(m_ref)

    fwd = d == 0
    gc = gc_ref[0]
    gr = gr_ref[0]
    r_i = lax.broadcasted_iota(jnp.int32, (T, T), 0)
    c_i = lax.broadcasted_iota(jnp.int32, (T, T), 1)
    allowed = jnp.where(fwd, r_i, c_i) >= jnp.where(fwd, c_i, r_i)
    tri = allowed.astype(F32)
    hi = lax.Precision.HIGHEST
    b_cols = jnp.dot(tri, gc, precision=hi, preferred_element_type=F32)
    b_rows = lax.dot_general(gr, tri, (((1,), (1,)), ((), ())), precision=hi,
                             preferred_element_type=F32)
    for h in range(H):
        pick_c = lambda a, col: jnp.where(fwd, a[:, col:col + 1], a[:, 8 + col:9 + col])
        pick_r = lambda a, rw: jnp.where(fwd, a[rw:rw + 1], a[8 + rw:9 + rw])
        i_col, f_col, bc = pick_c(gc, h), pick_c(gc, H + h), pick_c(b_cols, H + h)
        i_row, br = pick_r(gr, h), pick_r(b_rows, H + h)
        m_prev = m_ref[h:h + 1, 0:1]
        log_d = jnp.where(allowed, bc - br + i_row, NEG)
        log_inter = bc + m_prev
        m_t = jnp.maximum(log_inter, jnp.max(log_d, axis=-1, keepdims=True))
        qh = q_ref[0, :, h * dh:(h + 1) * dh]
        kh = k_ref[0, :, h * dh:(h + 1) * dh]
        vh = v_ref[0, :, h * dh:(h + 1) * dh]
        w_intra = _dot_nt(qh, kh) * jnp.exp(log_d - m_t)
        w_inter = jnp.exp(log_inter - m_t)
        ct = ct_ref[h]
        ones = jnp.ones((T, 128), MXU_DTYPE)
        both = _dot(w_intra, jnp.concatenate([vh, ones], axis=1)) + w_inter * _dot(qh, ct)
        inv = 1.0 / jnp.maximum(jnp.abs(both[:, dh:]), jnp.exp(-m_t))
        for c in range(dh // 128):
            o_ref[0, 0, :, h * dh + c * 128:h * dh + (c + 1) * 128] = (
                both[:, c * 128:(c + 1) * 128] * inv).astype(o_ref.dtype)
        b_end = jnp.sum(f_col, axis=0, keepdims=True)
        log_w = b_end - bc + i_col
        m_new = jnp.maximum(b_end + m_prev, jnp.max(log_w, axis=0, keepdims=True))
        w = jnp.exp(log_w - m_new)
        decay = jnp.exp(b_end + m_prev - m_new)
        wv = jnp.concatenate([w * vh.astype(F32), jnp.broadcast_to(w, (T, 128))], axis=1)
        ct_ref[h] = decay * ct + _dot_tn(kh, wv)
        m_ref[h:h + 1, :] = jnp.broadcast_to(m_new, (1, m_ref.shape[1]))


def _ml_post_kernel(x_ref, hf_ref, hb_ref, o_ref, mod_ref, ng_ref, w_ref, out_ref, *, D):
    gate = mod_ref[0, 0][2:3]
    dh = D // ML_HEADS
    hs = hf_ref[0, 0].astype(F32) + hb_ref[0, 0].astype(F32)
    parts = []
    for h in range(ML_HEADS):
        seg = hs[:, h * dh:(h + 1) * dh]
        parts.append(seg * lax.rsqrt(jnp.mean(seg * seg, axis=-1, keepdims=True) + EPS))
    hn = jnp.concatenate(parts, axis=1) * ng_ref[...]
    out_ref[0] = x_ref[0] + gate * _dot(o_ref[0].astype(F32) * hn, w_ref[...])


def _rope_tables(L, nctx, dh):
    half = dh // 2
    nf = half // 2
    t = jnp.arange(L, dtype=jnp.int32)
    inv = ROPE_BASE ** (-jnp.arange(nf, dtype=F32) / nf)

    def axis(pos):
        ang = pos.astype(F32)[:, None] * inv[None, :]
        c, s = jnp.cos(ang), jnp.sin(ang)
        return jnp.concatenate([c, c], axis=1), jnp.concatenate([-s, s], axis=1)

    cr, sr = axis(t // GRID_W)
    cc, sc = axis(t % GRID_W)
    cos = jnp.concatenate([cr, cc], axis=1)
    sin = jnp.concatenate([sr, sc], axis=1)
    cos = jnp.concatenate([jnp.ones((nctx, dh), F32), cos], axis=0)
    sin = jnp.concatenate([jnp.zeros((nctx, dh), F32), sin], axis=0)
    return cos, sin


def _ml_layer(xs, mod, g, w_in, conv_w, w_if, b_if, norm_g, w_out):
    B, S, D = xs.shape
    H = ML_HEADS
    dh = D // H
    nt = S // TS
    main, prev, nxt, mspec = _tile_specs(D, nt)
    cos, sin = _rope_tables(S - TS, TS, dh)
    wif = jnp.concatenate([w_if[0], w_if[1]], axis=1)
    bif = jnp.concatenate([b_if[0], b_if[1]], axis=0)
    wif_pad = jnp.zeros((D, 128), F32).at[:, :4 * H].set(wif).astype(MXU_DTYPE)
    bif_pad = jnp.zeros((1, 128), F32).at[0, :4 * H].set(bif)
    tabspec = pl.BlockSpec((TS, dh), lambda b, t: (t, 0))
    bf = lambda: jax.ShapeDtypeStruct(xs.shape, MXU_DTYPE)
    q, k, v, o, gc, gr = pl.pallas_call(
        functools.partial(_ml_pre_kernel, D=D, nt=nt),
        grid=(B, nt),
        in_specs=[main, prev, nxt, mspec, _const_spec((1, D)), _const_spec((D, 4 * D)), _const_spec((3, 2 * D)),
                  tabspec, tabspec, _const_spec((D, 128)), _const_spec((1, 128)),
                  _const_spec((4 * H, D)), _const_spec((4 * H, 1))],
        out_specs=[main, main, main, main,
                   pl.BlockSpec((1, TS, 128), lambda b, t: (b, t, 0)),
                   pl.BlockSpec((1, 4 * H, TS), lambda b, t: (b, 0, t))],
        out_shape=[bf(), bf(), bf(), bf(),
                   jax.ShapeDtypeStruct((B, S, 128), F32), jax.ShapeDtypeStruct((B, 4 * H, S), F32)],
        scratch_shapes=[pltpu.VMEM((2 * H, TS + 2 * HALO, dh), F32)],
        compiler_params=_cparams(("parallel", "parallel")),
        name="mlstm_pre",
    )(xs, xs, xs, mod, g.reshape(1, D), w_in.astype(MXU_DTYPE), conv_w, cos, sin,
      wif_pad, bif_pad, wif.T.astype(MXU_DTYPE), bif.reshape(4 * H, 1))

    def blk(d, j):
        return jnp.where(d == 0, j, jnp.where(j == 0, 0, nt - j))

    cspec = pl.BlockSpec((1, TS, D), lambda b, d, j: (b, blk(d, j), 0))
    hs = pl.pallas_call(
        functools.partial(_ml_scan_kernel, D=D),
        grid=(B, 2, nt),
        in_specs=[cspec, cspec, cspec,
                  pl.BlockSpec((1, TS, 128), lambda b, d, j: (b, blk(d, j), 0)),
                  pl.BlockSpec((1, 4 * H, TS), lambda b, d, j: (b, 0, blk(d, j)))],
        out_specs=pl.BlockSpec((1, 1, TS, D), lambda b, d, j: (d, b, blk(d, j), 0)),
        out_shape=jax.ShapeDtypeStruct((2, B, S, D), MXU_DTYPE),
        scratch_shapes=[pltpu.VMEM((H, dh, dh + 128), F32), pltpu.VMEM((8, 128), F32)],
        compiler_params=_cparams(("parallel", "arbitrary", "arbitrary")),
        name="mlstm_scan",
    )(q, k, v, gc, gr)
    dspec = lambda dd: pl.BlockSpec((1, 1, TS, D), lambda b, t: (dd, b, t, 0))
    return pl.pallas_call(
        functools.partial(_ml_post_kernel, D=D),
        grid=(B, nt),
        in_specs=[main, dspec(0), dspec(1), main, mspec, _const_spec((1, D)), _const_spec((D, D))],
        out_specs=main,
        out_shape=jax.ShapeDtypeStruct(xs.shape, F32),
        compiler_params=_cparams(("parallel", "parallel")),
        name="mlstm_post",
    )(xs, hs, hs, o, mod, norm_g.reshape(1, D), w_out.astype(MXU_DTYPE))


def _na_pre_kernel(x_ref, mod_ref, g_ref, w_ref, q_ref, k_ref, v_ref, *, D):
    mod = mod_ref[0, 0]
    h = _norm_mod(x_ref[0], g_ref[...], mod[1:2], mod[0:1]).astype(MXU_DTYPE)
    scale = LOG2E * (D // NA_HEADS) ** -0.5
    q_ref[0] = (_dot(h, w_ref[:, 0:D]) * scale).astype(q_ref.dtype)
    k_ref[0] = _dot(h, w_ref[:, D:2 * D]).astype(k_ref.dtype)
    v_ref[0] = _dot(h, w_ref[:, 2 * D:3 * D]).astype(v_ref.dtype)


def _na_attn_kernel(q_ref, k_ref, v_ref, bias_ref, o_ref, s_ref, m_ref, *, nctx, rows):
    t = pl.program_id(1)
    W = GRID_W
    nloc = NA_WIN_ROWS * W
    nq = nctx // W
    npairs = NA_HEADS // 2
    lane = lax.broadcasted_iota(jnp.int32, (W, 128), 1)
    r_i = lax.broadcasted_iota(jnp.int32, (2 * W, 128), 0)
    l_i = lax.broadcasted_iota(jnp.int32, (2 * W, 128), 1)
    same_head = jnp.right_shift(r_i, 6) == jnp.right_shift(l_i, 6)
    r0 = jnp.clip(t - nq - NA_WIN_ROWS // 2, 0, rows - NA_WIN_ROWS)
    base = pl.multiple_of(nctx + r0 * W, W)

    def scores(p, local):
        sl = slice(p * 128, (p + 1) * 128)
        qp = q_ref[0, :, sl]
        qs = jnp.concatenate([qp, qp], axis=0)
        qs = jnp.where(same_head, qs, jnp.zeros_like(qs))
        s_ctx = _dot_nt(k_ref[0, 0:nctx, sl], qs)
        s_ref[p, 0:nctx, :] = s_ctx
        m = jnp.max(s_ctx, axis=0, keepdims=True)
        if local:
            s_loc = _dot_nt(k_ref[0, pl.ds(base, nloc), sl], qs) + bias_ref[0, p]
            s_ref[p, nctx:nctx + nloc, :] = s_loc
            m = jnp.maximum(m, jnp.max(s_loc, axis=0, keepdims=True))
        m_ref[p, 0:1, :] = m

    def attend(p, local):
        sl = slice(p * 128, (p + 1) * 128)
        m = m_ref[p, 0:1, :]
        p_ctx = jnp.exp2(s_ref[p, 0:nctx, :] - m).astype(MXU_DTYPE)
        v_ctx = jnp.concatenate([v_ref[0, 0:nctx, sl], jnp.ones((nctx, 128), MXU_DTYPE)], axis=1)
        res = _dot_tn(p_ctx, v_ctx)
        if local:
            p_loc = jnp.exp2(s_ref[p, nctx:nctx + nloc, :] - m).astype(MXU_DTYPE)
            v_loc = jnp.concatenate([v_ref[0, pl.ds(base, nloc), sl], jnp.ones((nloc, 128), MXU_DTYPE)], axis=1)
            res = res + _dot_tn(p_loc, v_loc)
        out = res[:, 0:128] / res[:, 128:256]
        o_ref[0, :, sl] = jnp.where(lane < W, out[0:W], out[W:2 * W]).astype(o_ref.dtype)

    for local in (False, True):
        @pl.when((t >= nq) if local else (t < nq))
        def _():
            for p in range(npairs):
                scores(p, local)
            for p in range(npairs):
                attend(p, local)


def _proj_residual_kernel(x_ref, a_ref, mod_ref, w_ref, o_ref):
    gate = mod_ref[0, 0][2:3]
    o_ref[0] = x_ref[0] + gate * _dot(a_ref[0], w_ref[...])


def _na_bias_tables(rpb):
    H = rpb.shape[0]
    W, KR, KC = GRID_W, NA_WIN_ROWS, NA_WIN_COLS
    cols = jnp.arange(W, dtype=jnp.int32)
    c0 = jnp.clip(cols - KC // 2, 0, W - KC)
    col_mask = (cols[:, None] >= c0[None, :]) & (cols[:, None] < c0[None, :] + KC)
    dc_idx = jnp.clip(cols[:, None] - cols[None, :], 1 - KC, KC - 1) + KC - 1
    onehot = (dc_idx[:, :, None] == jnp.arange(2 * KC - 1)[None, None, :]).astype(F32)
    full = jnp.einsum('hrc,kqc->hrkq', rpb.astype(F32), onehot, precision=lax.Precision.HIGHEST)
    full = jnp.where(col_mask[None, None], full, NEG)
    tabs = []
    for off in range(KR):
        t = full[:, KR - 1 - off:2 * KR - 1 - off].reshape(H // 2, 2, KR, W, W)
        tabs.append(jnp.transpose(t, (0, 2, 3, 1, 4)).reshape(H // 2, KR * W, 2 * W))
    return jnp.stack(tabs, axis=0) * LOG2E


def _na_layer(xs, mod, g, w_qkv, rpb, w_out):
    B, S, D = xs.shape
    nt = S // TS
    W = GRID_W
    rows = (S - TS) // W
    assert rows >= NA_WIN_ROWS
    main, _, _, mspec = _tile_specs(D, nt)
    bf = lambda: jax.ShapeDtypeStruct(xs.shape, MXU_DTYPE)
    q, k, v = pl.pallas_call(
        functools.partial(_na_pre_kernel, D=D),
        grid=(B, nt),
        in_specs=[main, mspec, _const_spec((1, D)), _const_spec((D, 3 * D))],
        out_specs=[main, main, main],
        out_shape=[bf(), bf(), bf()],
        compiler_params=_cparams(("parallel", "parallel")),
        name="na_qkv",
    )(xs, mod, g.reshape(1, D), w_qkv.astype(MXU_DTYPE))
    bias = _na_bias_tables(rpb)
    nq = TS // W

    def variant(t):
        r = t - nq
        r0 = jnp.clip(r - NA_WIN_ROWS // 2, 0, rows - NA_WIN_ROWS)
        return jnp.where(t < nq, 0, r - r0)

    qspec = pl.BlockSpec((1, W, D), lambda b, t: (b, t, 0))
    full = pl.BlockSpec((1, S, D), lambda b, t: (b, 0, 0))
    o = pl.pallas_call(
        functools.partial(_na_attn_kernel, nctx=TS, rows=rows),
        grid=(B, S // W),
        in_specs=[qspec, full, full,
                  pl.BlockSpec((1, NA_HEADS // 2, NA_WIN_ROWS * W, 2 * W), lambda b, t: (variant(t), 0, 0, 0))],
        out_specs=qspec,
        out_shape=bf(),
        scratch_shapes=[pltpu.VMEM((NA_HEADS // 2, TS + NA_WIN_ROWS * W, 128), F32),
                        pltpu.VMEM((NA_HEADS // 2, 8, 128), F32)],
        compiler_params=_cparams(("parallel", "arbitrary")),
        name="na_attention",
    )(q, k, v, bias)
    return pl.pallas_call(
        _proj_residual_kernel,
        grid=(B, nt),
        in_specs=[main, main, mspec, _const_spec((D, D))],
        out_specs=main,
        out_shape=jax.ShapeDtypeStruct(xs.shape, F32),
        compiler_params=_cparams(("parallel", "parallel")),
        name="na_out",
    )(xs, o, mod, w_out.astype(MXU_DTYPE))


def kernel(x, c, ctx, c_ctx, ada_w, ada_b, norm1_g, norm2_g, ffn_w_in, ffn_conv, ffn_w_out, s5_lam_re, s5_lam_im, s5_log_dt, s5_b_re, s5_b_im, s5_c_re, s5_c_im, s5_d, s5_glu_w, s5_glu_b, ml_w_in, ml_conv, ml_w_if, ml_b_if, ml_norm_g, ml_w_out, na_w_qkv, na_rpb, na_w_out, final_g):
    assert ctx.shape[1] == TS and x.shape[1] % TS == 0 and x.shape[0] % 8 == 0
    depth = ada_w.shape[0]
    xs = jnp.concatenate([ctx, x], axis=1)
    mods = _modulation_all(c, c_ctx, ada_w, ada_b)
    for i in range(depth):
        kind, j = i % 3, i // 3
        if kind == 0:
            params = (s5_lam_re[j], s5_lam_im[j], s5_log_dt[j], s5_b_re[j], s5_b_im[j], s5_c_re[j], s5_c_im[j],
                      s5_d[j])
            xs = _s5_layer(xs, mods[i], norm1_g[i], params, s5_glu_w[j], s5_glu_b[j])
        elif kind == 1:
            xs = _ml_layer(xs, mods[i], norm1_g[i], ml_w_in[j], ml_conv[j], ml_w_if[j], ml_b_if[j],
                           ml_norm_g[j], ml_w_out[j])
        else:
            xs = _na_layer(xs, mods[i], norm1_g[i], na_w_qkv[j], na_rpb[j], na_w_out[j])
        xs = _ffn_layer(xs, mods[i], norm2_g[i], ffn_w_in[i], ffn_conv[i], ffn_w_out[i],
                        final_g if i == depth - 1 else None)
    return xs
```
